```python
import math
import jax
import jax.numpy as jnp
from jax import lax
import numpy as np

D_MODEL = 1024
BATCH = 4
SEQ = 4096
DEPTH = 4

GRID_W = 64
CTX_LEN = 256
D_MIX = D_MODEL
D_FF = 2816
N_MOD = 9
EPS = 1e-6

GLA_H = 4
GLA_DK = 32
GLA_DV = 64
GLA_LORA = 16
GLA_TAU = 16.0
GLA_CHUNK = 64
SSM_H = 4
SSM_P = 64
SSM_G = 2
SSM_N = 64
SSM_CONV = 3
SSM_CHUNK = 64
SSM_DI = SSM_H * SSM_P
SSM_CONV_CH = SSM_DI + 2 * SSM_G * SSM_N
RWKV_H = 4
RWKV_N = 64
RWKV_D = RWKV_H * RWKV_N
RWKV_W_LORA = 64
RWKV_A_LORA = 64
RWKV_G_LORA = 128
RWKV_GN_EPS = 64e-5
ATT_HQ = 4
ATT_HKV = 2
ATT_GROUP = ATT_HQ // ATT_HKV
ATT_HD = 64
ATT_BLOCK = 128
ROPE_THETA = 10000.0
ROPE_AXIS_DIM = ATT_HD // 2

GLA_COLS = [GLA_H * GLA_DK, GLA_H * GLA_DK, GLA_H * GLA_DV, GLA_H * GLA_DV, GLA_LORA]
SSM_COLS = [SSM_DI, SSM_CONV_CH, SSM_H]
RWKV_COLS = [RWKV_D, RWKV_D, RWKV_D, RWKV_W_LORA, RWKV_A_LORA, RWKV_G_LORA]
ATT_COLS = [ATT_HQ * ATT_HD, ATT_HKV * ATT_HD, ATT_HKV * ATT_HD]
GROUP_COLS = [sum(GLA_COLS), sum(SSM_COLS), sum(RWKV_COLS), sum(ATT_COLS)]
N_IN = sum(GROUP_COLS)

kernel_name = 'hybrid_parallel_group_dit_trunk'


def split_cols(u, sizes):
    out, start = [], 0
    for s in sizes:
        out.append(u[..., start:start + s])
        start += s
    return out


def rms_norm(x, g, eps=EPS):
    xf = x.astype(jnp.float32)
    y = xf * lax.rsqrt(jnp.mean(xf * xf, axis=-1, keepdims=True) + eps)
    return (y * g.astype(jnp.float32)).astype(x.dtype)


def modulate(h, g, shift, scale):
    return rms_norm(h, g) * (1 + scale[:, None]) + shift[:, None]


def swiglu(h, w_in, w_out):
    a, b = jnp.split(h @ w_in, 2, axis=-1)
    return (jax.nn.silu(a) * b) @ w_out


def _ident(t):
    return t


def _flip(t):
    return jnp.flip(t, axis=1)


def shift_prev(u):
    return jnp.pad(u, ((0, 0), (1, 0), (0, 0)))[:, :-1]


def shift_next(u):
    return jnp.pad(u, ((0, 0), (0, 1), (0, 0)))[:, 1:]


def dwconv_centred(u, w, b):
    pad = w.shape[0] // 2
    y = lax.conv_general_dilated(u, w[:, None, :].astype(u.dtype), window_strides=(1,),
                                 padding=[(pad, pad)], dimension_numbers=('NWC', 'WIO', 'NWC'),
                                 feature_group_count=u.shape[-1])
    return y + b


def axial_rope_tables(n_tokens, dtype):
    rows = n_tokens // GRID_W
    f32 = jnp.float32
    row = jnp.repeat(jnp.arange(rows, dtype=f32), GRID_W)
    col = jnp.tile(jnp.arange(GRID_W, dtype=f32), rows)
    inv = ROPE_THETA ** (-jnp.arange(0, ROPE_AXIS_DIM, 2, dtype=f32) / ROPE_AXIS_DIM)
    ang = jnp.stack([row[:, None] * inv, col[:, None] * inv], axis=1)
    return jnp.cos(ang).astype(dtype), jnp.sin(ang).astype(dtype)


def apply_axial_rope(t, cos, sin):
    bsz, n, h, d = t.shape
    t = t.reshape(bsz, n, h, 2, 2, ROPE_AXIS_DIM // 2)
    t1, t2 = t[..., 0, :], t[..., 1, :]
    c, s = cos[None, :, None], sin[None, :, None]
    out = jnp.stack([t1 * c - t2 * s, t1 * s + t2 * c], axis=-2)
    return out.reshape(bsz, n, h, d)


def gla_chunked(q, k, v, log_a, s0):
    bsz, t, h, dk = q.shape
    dv = v.shape[-1]
    cs = GLA_CHUNK
    nc = t // cs
    q, k, v, log_a = [z.reshape(bsz, nc, cs, h, z.shape[-1]) for z in (q, k, v, log_a)]
    b = jnp.cumsum(log_a, axis=2)
    b_last = b[:, :, -1:]
    q_dec = q * jnp.exp(b)
    causal = jnp.tril(jnp.ones((cs, cs), bool))
    att = jnp.einsum('bnthd,bnshd->bnhts', q_dec, k * jnp.exp(-b))
    att = jnp.where(causal, att, 0.0)
    o_intra = jnp.einsum('bnhts,bnshv->bnthv', att, v)
    chunk_state = jnp.einsum('bnshd,bnshv->bnhdv', k * jnp.exp(b_last - b), v)
    chunk_decay = jnp.exp(b_last[:, :, 0])

    def step(s, inp):
        st, dec = inp
        return s * dec[..., None] + st, s

    s_fin, s_prev = lax.scan(step, s0, (jnp.moveaxis(chunk_state, 1, 0), jnp.moveaxis(chunk_decay, 1, 0)))
    s_prev = jnp.moveaxis(s_prev, 0, 1)
    o_inter = jnp.einsum('bnthd,bnhdv->bnthv', q_dec, s_prev)
    return (o_intra + o_inter).reshape(bsz, t, h, dv), s_fin


def gla_mixer(u_c, u_x, w_dec, b_dec, norm_g, need_ctx):
    f32 = jnp.float32

    def streams(u):
        bsz, t, _ = u.shape
        q, k, v, g, lr = split_cols(u, GLA_COLS)
        q = q.reshape(bsz, t, GLA_H, GLA_DK).astype(f32) * (GLA_DK ** -0.5)
        k = k.reshape(bsz, t, GLA_H, GLA_DK).astype(f32)
        v = v.reshape(bsz, t, GLA_H, GLA_DV).astype(f32)
        return q, k, v, g, lr

    def log_decay(lr, d):
        z = (lr @ w_dec[d] + b_dec[d]).astype(f32)
        return (jax.nn.log_sigmoid(z) / GLA_TAU).reshape(lr.shape[0], lr.shape[1], GLA_H, GLA_DK)

    def finish(o, g):
        bsz, t = o.shape[:2]
        o = rms_norm(o, norm_g).reshape(bsz, t, GLA_H * GLA_DV)
        return o.astype(g.dtype) * jax.nn.silu(g)

    qc, kc, vc, gc, lc = streams(u_c)
    qx, kx, vx, gx, lx = streams(u_x)
    bsz = u_x.shape[0]
    o_c = jnp.zeros_like(vc)
    o_x = jnp.zeros_like(vx)
    for d in range(2):
        fl = _flip if d else _ident
        s0 = jnp.zeros((bsz, GLA_H, GLA_DK, GLA_DV), f32)
        oc, s_ctx = gla_chunked(fl(qc), fl(kc), fl(vc), fl(log_decay(lc, d)), s0)
        ox, _ = gla_chunked(fl(qx), fl(kx), fl(vx), fl(log_decay(lx, d)), s_ctx)
        o_c = o_c + fl(oc)
        o_x = o_x + fl(ox)
    y_c = finish(o_c, gc) if need_ctx else None
    return y_c, finish(o_x, gx)


def ssd_chunked(xs, dt, a, bm, cm, h0):
    bsz, t, h, p = xs.shape
    g, n = bm.shape[-2:]
    r = h // g
    cs = SSM_CHUNK
    nc = t // cs
    xdt = (xs * dt[..., None]).reshape(bsz, nc, cs, g, r, p)
    cum = jnp.cumsum((dt * a).reshape(bsz, nc, cs, g, r), axis=2)
    bc = bm.reshape(bsz, nc, cs, g, n)
    cc = cm.reshape(bsz, nc, cs, g, n)
    cum_t = jnp.moveaxis(cum, 2, -1)
    mask = jnp.tril(jnp.ones((cs, cs), bool))
    seg = jnp.exp(jnp.where(mask, cum_t[..., :, None] - cum_t[..., None, :], -jnp.inf))
    cb = jnp.einsum('bctgn,bcsgn->bcgts', cc, bc)
    y_intra = jnp.einsum('bcgts,bcgrts,bcsgrp->bctgrp', cb, seg, xdt)
    last = cum[:, :, -1]
    to_end = jnp.exp(last[:, :, None] - cum)
    chunk_state = jnp.einsum('bcsgn,bcsgr,bcsgrp->bcgrpn', bc, to_end, xdt)

    def step(hs, inp):
        st, dec = inp
        return hs * dec[..., None, None] + st, hs

    h_fin, h_prev = lax.scan(step, h0, (jnp.moveaxis(chunk_state, 1, 0), jnp.moveaxis(jnp.exp(last), 1, 0)))
    h_prev = jnp.moveaxis(h_prev, 0, 1)
    y_inter = jnp.einsum('bctgn,bctgr,bcgrpn->bctgrp', cc, jnp.exp(cum), h_prev)
    return (y_intra + y_inter).reshape(bsz, t, h, p), h_fin


def mamba_mixer(u_c, u_x, conv_w, conv_b, dt_bias, a_log, d_skip, norm_g, need_ctx):
    f32 = jnp.float32

    def streams(u):
        bsz, t, _ = u.shape
        z, xbc, dt = split_cols(u, SSM_COLS)
        xbc = jax.nn.silu(dwconv_centred(xbc, conv_w, conv_b))
        xs, bm, cm = split_cols(xbc, [SSM_DI, SSM_G * SSM_N, SSM_G * SSM_N])
        return (z, xs.reshape(bsz, t, SSM_H, SSM_P).astype(f32),
                bm.reshape(bsz, t, SSM_G, SSM_N).astype(f32),
                cm.reshape(bsz, t, SSM_G, SSM_N).astype(f32), dt.astype(f32))

    def finish(y, z, xs):
        bsz, t = y.shape[:2]
        y = y + d_skip.astype(f32)[:, None] * xs
        y = y.reshape(bsz, t, SSM_DI) * jax.nn.silu(z.astype(f32))
        return rms_norm(y, norm_g).astype(z.dtype)

    zc, xc, bc, cc, dtc = streams(u_c)
    zx, xx, bx, cx, dtx = streams(u_x)
    bsz = u_x.shape[0]
    y_c = jnp.zeros_like(xc)
    y_x = jnp.zeros_like(xx)
    for d in range(2):
        fl = _flip if d else _ident
        a = -jnp.exp(a_log[d].astype(f32))
        bias = dt_bias[d].astype(f32)
        h0 = jnp.zeros((bsz, SSM_G, SSM_H // SSM_G, SSM_P, SSM_N), f32)
        yc, h_ctx = ssd_chunked(fl(xc), fl(jax.nn.softplus(dtc + bias)), a, fl(bc), fl(cc), h0)
        yx, _ = ssd_chunked(fl(xx), fl(jax.nn.softplus(dtx + bias)), a, fl(bx), fl(cx), h_ctx)
        y_c = y_c + fl(yc)
        y_x = y_x + fl(yx)
    out_c = finish(y_c, zc, xc) if need_ctx else None
    return out_c, finish(y_x, zx, xx)


def rwkv7_scan(r, w, k, v, a_in, b_in, s0):
    def step(s, inp):
        r_t, w_t, k_t, v_t, a_t, b_t = inp
        sa = jnp.einsum('bhij,bhj->bhi', s, a_t)
        s = s * w_t[:, :, None, :] + sa[..., None] * b_t[:, :, None, :] + v_t[..., None] * k_t[:, :, None, :]
        return s, jnp.einsum('bhij,bhj->bhi', s, r_t)

    seq = tuple(jnp.moveaxis(z, 1, 0) for z in (r, w, k, v, a_in, b_in))
    s_fin, y = lax.scan(step, s0, seq)
    return jnp.moveaxis(y, 0, 1), s_fin


def rwkv_mixer(u_c, u_x, shift_mu, w0, w_dec, a0, w_a, w_g, k_k, k_a, r_k, ln_g, ln_b, need_ctx):
    f32 = jnp.float32

    def streams(u):
        bsz, t, _ = u.shape
        u = u + shift_mu[0] * (shift_prev(u) - u) + shift_mu[1] * (shift_next(u) - u)
        r, k, v, lw, la, lg = split_cols(u, RWKV_COLS)
        heads = lambda z: z.reshape(bsz, t, RWKV_H, RWKV_N).astype(f32)
        a = jax.nn.sigmoid(a0 + la @ w_a)
        g = jax.nn.sigmoid(lg) @ w_g
        kk = heads(k * k_k)
        kk = kk / jnp.maximum(jnp.sqrt(jnp.sum(kk * kk, axis=-1, keepdims=True)), 1e-12)
        k = k * (1 + (a - 1) * k_a)
        return heads(r), heads(k), heads(v), heads(a), kk, jnp.tanh(lw), g

    def log_decay(tlw, d):
        wr = (w0[d] + tlw @ w_dec[d]).astype(f32)
        wr = -jax.nn.softplus(-wr) - 0.5
        return (-jnp.exp(wr)).reshape(tlw.shape[0], tlw.shape[1], RWKV_H, RWKV_N)

    def finish(y, r, k, v, g):
        bsz, t = y.shape[:2]
        mu = jnp.mean(y, axis=-1, keepdims=True)
        var = jnp.mean(jnp.square(y - mu), axis=-1, keepdims=True)
        y = ((y - mu) * lax.rsqrt(var + RWKV_GN_EPS)).reshape(bsz, t, RWKV_D) * ln_g + ln_b
        bonus = jnp.sum(r * k * r_k.astype(f32), axis=-1, keepdims=True) * v
        y = y + bonus.reshape(bsz, t, RWKV_D)
        return (y * g.astype(f32)).astype(g.dtype)

    rc, kc, vc, ac, kkc, tc, gc = streams(u_c)
    rx, kx, vx, ax, kkx, tx, gx = streams(u_x)
    bsz = u_x.shape[0]
    y_c = jnp.zeros_like(vc)
    y_x = jnp.zeros_like(vx)
    for d in range(2):
        fl = _flip if d else _ident
        s0 = jnp.zeros((bsz, RWKV_H, RWKV_N, RWKV_N), f32)
        yc, s_ctx = rwkv7_scan(fl(rc), fl(jnp.exp(log_decay(tc, d))), fl(kc), fl(vc), fl(-kkc), fl(kkc * ac), s0)
        yx, _ = rwkv7_scan(fl(rx), fl(jnp.exp(log_decay(tx, d))), fl(kx), fl(vx), fl(-kkx), fl(kkx * ax), s_ctx)
        y_c = y_c + fl(yc)
        y_x = y_x + fl(yx)
    out_c = finish(y_c, rc, kc, vc, gc) if need_ctx else None
    return out_c, finish(y_x, rx, kx, vx, gx)


def gqa_attend(q, k, v):
    s = jnp.einsum('bqkgd,bskd->bkgqs', q, k).astype(jnp.float32) * (ATT_HD ** -0.5)
    p = jax.nn.softmax(s, axis=-1)
    return jnp.einsum('bkgqs,bskd->bqkgd', p.astype(v.dtype), v)


def attention_mixer(u_c, u_x, q_norm, k_norm, cos, sin, need_ctx):
    def streams(u):
        bsz, t, _ = u.shape
        q, k, v = split_cols(u, ATT_COLS)
        q = rms_norm(q.reshape(bsz, t, ATT_HQ, ATT_HD), q_norm)
        k = rms_norm(k.reshape(bsz, t, ATT_HKV, ATT_HD), k_norm)
        return q, k, v.reshape(bsz, t, ATT_HKV, ATT_HD)

    qc, kc, vc = streams(u_c)
    qx, kx, vx = streams(u_x)
    qx = apply_axial_rope(qx, cos, sin)
    kx = apply_axial_rope(kx, cos, sin)
    keys = jnp.concatenate([kx, kc], axis=1)
    vals = jnp.concatenate([vx, vc], axis=1)
    bsz, t = qx.shape[:2]
    nb = t // ATT_BLOCK
    q_blocks = jnp.moveaxis(qx.reshape(bsz, nb, ATT_BLOCK, ATT_HKV, ATT_GROUP, ATT_HD), 1, 0)
    y_x = lax.map(lambda qb: gqa_attend(qb, keys, vals), q_blocks)
    y_x = jnp.moveaxis(y_x, 0, 1).reshape(bsz, t, ATT_HQ * ATT_HD)
    y_c = None
    if need_ctx:
        lc = qc.shape[1]
        y_c = gqa_attend(qc.reshape(bsz, lc, ATT_HKV, ATT_GROUP, ATT_HD), kc, vc).reshape(bsz, lc, ATT_HQ * ATT_HD)
    return y_c, y_x


def hybrid_layer(x, ctx, mod_x, mod_c, cos, sin, p, need_ctx):
    mx = jnp.split(mod_x, N_MOD, axis=-1)
    mc = jnp.split(mod_c, N_MOD, axis=-1)
    x = x + 0.5 * mx[2][:, None] * swiglu(modulate(x, p['norm_g'][0], mx[0], mx[1]), p['ffn_in'][0], p['ffn_out'][0])
    ctx = ctx + 0.5 * mc[2][:, None] * swiglu(modulate(ctx, p['norm_g'][0], mc[0], mc[1]), p['ffn_in'][0], p['ffn_out'][0])
    ux = modulate(x, p['norm_g'][1], mx[3], mx[4]) @ p['w_in']
    uc = modulate(ctx, p['norm_g'][1], mc[3], mc[4]) @ p['w_in']
    gx = split_cols(ux, GROUP_COLS)
    gc = split_cols(uc, GROUP_COLS)
    gla_c, gla_x = gla_mixer(gc[0], gx[0], p['gla_w_dec'], p['gla_b_dec'], p['gla_norm'], need_ctx)
    ssm_c, ssm_x = mamba_mixer(gc[1], gx[1], p['ssm_conv_w'], p['ssm_conv_b'], p['ssm_dt_bias'],
                               p['ssm_a_log'], p['ssm_d'], p['ssm_norm'], need_ctx)
    rw_c, rw_x = rwkv_mixer(gc[2], gx[2], p['rwkv_shift_mu'], p['rwkv_w0'], p['rwkv_w_dec'], p['rwkv_a0'],
                            p['rwkv_w_a'], p['rwkv_w_g'], p['rwkv_k_k'], p['rwkv_k_a'], p['rwkv_r_k'],
                            p['rwkv_ln_g'], p['rwkv_ln_b'], need_ctx)
    at_c, at_x = attention_mixer(gc[3], gx[3], p['att_q_norm'], p['att_k_norm'], cos, sin, need_ctx)
    x = x + mx[5][:, None] * (jnp.concatenate([gla_x, ssm_x, rw_x, at_x], axis=-1) @ p['w_out'])
    if need_ctx:
        ctx = ctx + mc[5][:, None] * (jnp.concatenate([gla_c, ssm_c, rw_c, at_c], axis=-1) @ p['w_out'])
        ctx = ctx + 0.5 * mc[8][:, None] * swiglu(modulate(ctx, p['norm_g'][2], mc[6], mc[7]), p['ffn_in'][1], p['ffn_out'][1])
    x = x + 0.5 * mx[8][:, None] * swiglu(modulate(x, p['norm_g'][2], mx[6], mx[7]), p['ffn_in'][1], p['ffn_out'][1])
    return x, ctx


def setup_inputs(seed: int = 0) -> dict:
    key = jax.random.key(seed)
    keys = jax.random.split(key, 40)
    counter = [0]
    f32 = jnp.float32

    def nk():
        k = keys[counter[0]]
        counter[0] += 1
        return k

    def nrm(shape, scale):
        return jax.random.normal(nk(), shape, f32) * scale

    def unif(shape, lo, hi):
        return jax.random.uniform(nk(), shape, f32, lo, hi)

    L, D = DEPTH, D_MODEL
    x = nrm((BATCH, SEQ, D), 1.0)
    c = nrm((BATCH, D), 1.0)
    ctx = nrm((BATCH, CTX_LEN, D), 1.0)
    c_ctx = nrm((D,), 1.0)
    norm_g = 1.0 + nrm((L, 3, D), 0.1)
    w_mod = nrm((L, D, N_MOD * D), 0.5 * D ** -0.5)
    b_mod = nrm((L, N_MOD * D), 0.02)
    ffn_in = nrm((L, 2, D, 2 * D_FF), D ** -0.5)
    ffn_out = nrm((L, 2, D_FF, D), D_FF ** -0.5)
    w_in = nrm((L, D, N_IN), D ** -0.5)
    w_out = nrm((L, D_MIX, D), D_MIX ** -0.5)
    gla_w_dec = nrm((L, 2, GLA_LORA, GLA_H * GLA_DK), GLA_LORA ** -0.5)
    gla_b_dec = nrm((L, 2, GLA_H * GLA_DK), 0.5)
    gla_norm = 1.0 + nrm((L, GLA_DV), 0.1)
    ssm_conv_w = nrm((L, SSM_CONV, SSM_CONV_CH), SSM_CONV ** -0.5)
    ssm_conv_b = nrm((L, SSM_CONV_CH), 0.02)
    dt0 = jnp.exp(unif((L, 2, SSM_H), math.log(1e-3), math.log(1e-1)))
    ssm_dt_bias = dt0 + jnp.log(-jnp.expm1(-dt0))
    ssm_a_log = jnp.log(unif((L, 2, SSM_H), 1.0, 16.0))
    ssm_d = 1.0 + nrm((L, SSM_H), 0.1)
    ssm_norm = 1.0 + nrm((L, SSM_DI), 0.1)
    rwkv_shift_mu = unif((L, 2, sum(RWKV_COLS)), 0.0, 0.5)
    rwkv_w0 = unif((L, 2, RWKV_D), -3.0, 1.0)
    rwkv_w_dec = nrm((L, 2, RWKV_W_LORA, RWKV_D), 0.5 * RWKV_W_LORA ** -0.5)
    rwkv_a0 = nrm((L, RWKV_D), 0.1)
    rwkv_w_a = nrm((L, RWKV_A_LORA, RWKV_D), RWKV_A_LORA ** -0.5)
    rwkv_w_g = nrm((L, RWKV_G_LORA, RWKV_D), RWKV_G_LORA ** -0.5)
    rwkv_k_k = 0.85 + nrm((L, RWKV_D), 0.05)
    rwkv_k_a = 1.0 + nrm((L, RWKV_D), 0.05)
    rwkv_r_k = nrm((L, RWKV_H, RWKV_N), 0.1)
    rwkv_ln_g = 1.0 + nrm((L, RWKV_D), 0.1)
    rwkv_ln_b = nrm((L, RWKV_D), 0.02)
    att_q_norm = 1.0 + nrm((L, ATT_HD), 0.1)
    att_k_norm = 1.0 + nrm((L, ATT_HD), 0.1)
    return {'x': x, 'c': c, 'ctx': ctx, 'c_ctx': c_ctx, 'norm_g': norm_g, 'w_mod': w_mod, 'b_mod': b_mod,
            'ffn_in': ffn_in, 'ffn_out': ffn_out, 'w_in': w_in, 'w_out': w_out,
            'gla_w_dec': gla_w_dec, 'gla_b_dec': gla_b_dec, 'gla_norm': gla_norm,
            'ssm_conv_w': ssm_conv_w, 'ssm_conv_b': ssm_conv_b, 'ssm_dt_bias': ssm_dt_bias,
            'ssm_a_log': ssm_a_log, 'ssm_d': ssm_d, 'ssm_norm': ssm_norm,
            'rwkv_shift_mu': rwkv_shift_mu, 'rwkv_w0': rwkv_w0, 'rwkv_w_dec': rwkv_w_dec, 'rwkv_a0': rwkv_a0,
            'rwkv_w_a': rwkv_w_a, 'rwkv_w_g': rwkv_w_g, 'rwkv_k_k': rwkv_k_k, 'rwkv_k_a': rwkv_k_a,
            'rwkv_r_k': rwkv_r_k, 'rwkv_ln_g': rwkv_ln_g, 'rwkv_ln_b': rwkv_ln_b,
            'att_q_norm': att_q_norm, 'att_k_norm': att_k_norm}


def reference(x, c, ctx, c_ctx, norm_g, w_mod, b_mod, ffn_in, ffn_out, w_in, w_out,
              gla_w_dec, gla_b_dec, gla_norm, ssm_conv_w, ssm_conv_b, ssm_dt_bias, ssm_a_log, ssm_d, ssm_norm,
              rwkv_shift_mu, rwkv_w0, rwkv_w_dec, rwkv_a0, rwkv_w_a, rwkv_w_g, rwkv_k_k, rwkv_k_a, rwkv_r_k,
              rwkv_ln_g, rwkv_ln_b, att_q_norm, att_k_norm):
    cos, sin = axial_rope_tables(x.shape[1], x.dtype)
    sc = jax.nn.silu(c)
    scc = jax.nn.silu(c_ctx)[None]
    for l in range(DEPTH):
        mod_x = sc @ w_mod[l] + b_mod[l]
        mod_c = scc @ w_mod[l] + b_mod[l]
        p = dict(norm_g=norm_g[l], ffn_in=ffn_in[l], ffn_out=ffn_out[l], w_in=w_in[l], w_out=w_out[l],
                 gla_w_dec=gla_w_dec[l], gla_b_dec=gla_b_dec[l], gla_norm=gla_norm[l],
                 ssm_conv_w=ssm_conv_w[l], ssm_conv_b=ssm_conv_b[l], ssm_dt_bias=ssm_dt_bias[l],
                 ssm_a_log=ssm_a_log[l], ssm_d=ssm_d[l], ssm_norm=ssm_norm[l],
                 rwkv_shift_mu=rwkv_shift_mu[l], rwkv_w0=rwkv_w0[l], rwkv_w_dec=rwkv_w_dec[l],
                 rwkv_a0=rwkv_a0[l], rwkv_w_a=rwkv_w_a[l], rwkv_w_g=rwkv_w_g[l], rwkv_k_k=rwkv_k_k[l],
                 rwkv_k_a=rwkv_k_a[l], rwkv_r_k=rwkv_r_k[l], rwkv_ln_g=rwkv_ln_g[l], rwkv_ln_b=rwkv_ln_b[l],
                 att_q_norm=att_q_norm[l], att_k_norm=att_k_norm[l])
        x, ctx = hybrid_layer(x, ctx, mod_x, mod_c, cos, sin, p, l < DEPTH - 1)
    return x
```

```python
import functools
import math

import jax
import jax.numpy as jnp
from jax import lax
from jax.experimental import pallas as pl
from jax.experimental.pallas import tpu as pltpu

F32 = jnp.float32
BF16 = jnp.bfloat16

N_MOD = 9
EPS = 1e-6
GLA_H, GLA_DK, GLA_DV, GLA_LORA, GLA_TAU = 4, 32, 64, 16, 16.0
SSM_H, SSM_P, SSM_G, SSM_N = 4, 64, 2, 64
RWKV_H, RWKV_N, RWKV_GN_EPS = 4, 64, 64e-5
ATT_HQ, ATT_HKV, ATT_HD = 4, 2, 64
GRID_W = 64
ROPE_THETA = 10000.0
ROPE_AXIS_DIM = ATT_HD // 2

CHUNK = 64
SUB = 16
BLK = 256
NSUB = BLK // CHUNK
LANE = 128
HEAD_W = 256
VMEM_LIMIT = 56 * 1024 * 1024

GLA_W = 896
SSM_W = 896
RWKV_W = 1024
ATT_W = 512


def _cparams(sem):
    return pltpu.CompilerParams(dimension_semantics=sem, vmem_limit_bytes=VMEM_LIMIT)


def _mm(a, b):
    return jnp.dot(a.astype(BF16), b.astype(BF16), preferred_element_type=F32)


def _dg(a, b, dims):
    return lax.dot_general(a, b, (dims, ((), ())), preferred_element_type=F32)


_NN = ((1,), (0,))
_NT = ((1,), (1,))
_TN = ((0,), (0,))


def _split2(x):
    hi = x.astype(BF16)
    lo = (x - hi.astype(F32)).astype(BF16)
    return hi, lo


def _split3(x):
    hi = x.astype(BF16)
    r1 = x - hi.astype(F32)
    mid = r1.astype(BF16)
    lo = (r1 - mid.astype(F32)).astype(BF16)
    return hi, mid, lo


def _mm3(a, b, dims=_NN):
    ah, al = _split2(a)
    bh, bl = _split2(b)
    return _dg(ah, bh, dims) + (_dg(ah, bl, dims) + _dg(al, bh, dims))


def _mm_mask_lhs(mask_bf16, x, dims=_NN):
    hi, mid, lo = _split3(x)
    return _dg(mask_bf16, hi, dims) + (_dg(mask_bf16, mid, dims) + _dg(mask_bf16, lo, dims))


def _mm_mask_rhs(x, mask_bf16, dims=_NN):
    hi, mid, lo = _split3(x)
    return _dg(hi, mask_bf16, dims) + (_dg(mid, mask_bf16, dims) + _dg(lo, mask_bf16, dims))


def _sigmoid(x):
    return jax.nn.sigmoid(x)


def _silu(x):
    return x * jax.nn.sigmoid(x)


def _softplus(x):
    return jnp.maximum(x, 0.0) + jnp.log1p(jnp.exp(-jnp.abs(x)))


def _modulate(x, g, shift, scale):
    ms = jnp.mean(x * x, axis=-1, keepdims=True)
    return (x * lax.rsqrt(ms + EPS) * g) * (1.0 + scale) + shift


def _block_of_step(d, i, nbc, nb):
    back = jnp.where(i < nbc, nbc - 1 - i, nb + nbc - 1 - i)
    return jnp.where(d == 0, i, back)


def _mod_row(b, i, nbc):
    return jnp.where(i < nbc, 4, b)


def _mod_kernel(c_ref, w_ref, b_ref, o_ref):
    o_ref[0] = _mm(_silu(c_ref[...]), w_ref[0]) + b_ref[0]


def _compute_mod(cvec, w_mod, b_mod):
    depth, d, n = w_mod.shape
    tn = 1024
    return pl.pallas_call(
        _mod_kernel,
        grid=(depth, n // tn),
        in_specs=[pl.BlockSpec((8, d), lambda l, j: (0, 0)),
                  pl.BlockSpec((1, d, tn), lambda l, j: (l, 0, j)),
                  pl.BlockSpec((1, 1, tn), lambda l, j: (l, 0, j))],
        out_specs=pl.BlockSpec((1, 8, tn), lambda l, j: (l, 0, j)),
        out_shape=jax.ShapeDtypeStruct((depth, 8, n), F32),
        compiler_params=_cparams(("arbitrary", "arbitrary")),
        name="adaln_mod",
    )(cvec, w_mod, b_mod.reshape(depth, 1, n))


def _ffn_kernel(h_ref, mod_ref, g_ref, w1_ref, w2_ref, o_ref, *, i0, d_ff):
    x = h_ref[0]
    m = mod_ref[0]
    hm = _modulate(x, g_ref[...], m[i0:i0 + 1], m[i0 + 1:i0 + 2]).astype(BF16)
    ab = jnp.dot(hm, w1_ref[...], preferred_element_type=F32)
    a = ab[:, :d_ff]
    act = (_silu(a) * ab[:, d_ff:]).astype(BF16)
    y = jnp.dot(act, w2_ref[...], preferred_element_type=F32)
    o_ref[0] = x + (0.5 * m[i0 + 2:i0 + 3]) * y


def _ffn(h, mod, g, w1, w2, i0, nbc):
    bsz, t, d = h.shape
    d_ff = w2.shape[0]
    return pl.pallas_call(
        functools.partial(_ffn_kernel, i0=i0, d_ff=d_ff),
        grid=(bsz, t // BLK),
        in_specs=[pl.BlockSpec((1, BLK, d), lambda b, i: (b, i, 0)),
                  pl.BlockSpec((1, N_MOD, d), lambda b, i: (_mod_row(b, i, nbc), 0, 0)),
                  pl.BlockSpec((1, d), lambda b, i: (0, 0)),
                  pl.BlockSpec((d, 2 * d_ff), lambda b, i: (0, 0)),
                  pl.BlockSpec((d_ff, d), lambda b, i: (0, 0))],
        out_specs=pl.BlockSpec((1, BLK, d), lambda b, i: (b, i, 0)),
        out_shape=jax.ShapeDtypeStruct(h.shape, F32),
        compiler_params=_cparams(("arbitrary", "arbitrary")),
        name="ffn",
    )(h, mod, g, w1, w2)


def _inproj_kernel(h_ref, mod_ref, g_ref, w_ref, gla_ref, ssm_ref, rwkv_ref, att_ref):
    m = mod_ref[0]
    hm = _modulate(h_ref[0], g_ref[...], m[3:4], m[4:5]).astype(BF16)
    u = jnp.dot(hm, w_ref[...], preferred_element_type=F32)
    gla_ref[0] = u[:, :GLA_W]
    ssm_ref[0] = u[:, GLA_W:GLA_W + SSM_W]
    rwkv_ref[0] = u[:, GLA_W + SSM_W:GLA_W + SSM_W + RWKV_W]
    att_ref[0] = u[:, GLA_W + SSM_W + RWKV_W:]


def _inproj(h, mod, g, w, nbc):
    bsz, t, d = h.shape
    widths = (GLA_W, SSM_W, RWKV_W, ATT_W)
    return pl.pallas_call(
        _inproj_kernel,
        grid=(bsz, t // BLK),
        in_specs=[pl.BlockSpec((1, BLK, d), lambda b, i: (b, i, 0)),
                  pl.BlockSpec((1, N_MOD, d), lambda b, i: (_mod_row(b, i, nbc), 0, 0)),
                  pl.BlockSpec((1, d), lambda b, i: (0, 0)),
                  pl.BlockSpec((d, sum(widths)), lambda b, i: (0, 0))],
        out_specs=[pl.BlockSpec((1, BLK, w_), lambda b, i: (b, i, 0)) for w_ in widths],
        out_shape=[jax.ShapeDtypeStruct((bsz, t, w_), F32) for w_ in widths],
        compiler_params=_cparams(("arbitrary", "arbitrary")),
        name="inproj",
    )(h, mod, g, w)


def _gla_kernel(u_ref, wdec_ref, bdec_ref, tri_ref, o_ref, s_ref):
    d = pl.program_id(1)
    i = pl.program_id(2)

    @pl.when(i == 0)
    def _():
        s_ref[...] = jnp.zeros_like(s_ref)

    tri = tri_ref[d]
    tri_b = tri.astype(BF16)
    wdec = wdec_ref[0].astype(BF16)
    bdec = bdec_ref[0]

    def sub(j, carry):
        jj = jnp.where(d == 0, j, NSUB - 1 - j)
        r0 = pl.multiple_of(jj * CHUNK, CHUNK)
        u = u_ref[0, pl.ds(r0, CHUNK), :]
        q = u[:, 0:128] * (GLA_DK ** -0.5)
        k = u[:, 128:256]
        v = u[:, 256:512]
        z = _mm(u[:, 768:896], wdec) + bdec
        la = -_softplus(-z) * (1.0 / GLA_TAU)
        bc = _mm_mask_lhs(tri_b, la)
        bt = jnp.sum(la, axis=0, keepdims=True)
        qd = q * jnp.exp(bc)
        ki = k * jnp.exp(-bc)
        ke = k * jnp.exp(bt - bc)
        dec = jnp.exp(bt)
        for h in range(GLA_H):
            ks = slice(GLA_DK * h, GLA_DK * (h + 1))
            vs = slice(GLA_DV * h, GLA_DV * (h + 1))
            qh = qd[:, ks].astype(BF16)
            vh = v[:, vs].astype(BF16)
            st = s_ref[h]
            att = _dg(qh, ki[:, ks].astype(BF16), _NT) * tri
            oh = _mm(att, vh) + _dg(qh, st.astype(BF16), _NT)
            o_ref[0, 0, pl.ds(r0, CHUNK), vs] = oh
            s_ref[h] = st * dec[:, ks] + _dg(vh, ke[:, ks].astype(BF16), _TN)
        return carry

    lax.fori_loop(0, NSUB, sub, 0)


def _gla_scan(u, wdec, bdec, tri, nbc):
    bsz, t, _ = u.shape
    nb = t // BLK
    blk = lambda b, d, i: _block_of_step(d, i, nbc, nb)
    return pl.pallas_call(
        _gla_kernel,
        grid=(bsz, 2, nb),
        in_specs=[pl.BlockSpec((1, BLK, GLA_W), lambda b, d, i: (b, blk(b, d, i), 0)),
                  pl.BlockSpec((1, LANE, LANE), lambda b, d, i: (d, 0, 0)),
                  pl.BlockSpec((1, 1, LANE), lambda b, d, i: (d, 0, 0)),
                  pl.BlockSpec((2, CHUNK, CHUNK), lambda b, d, i: (0, 0, 0))],
        out_specs=pl.BlockSpec((1, 1, BLK, HEAD_W), lambda b, d, i: (d, b, blk(b, d, i), 0)),
        out_shape=jax.ShapeDtypeStruct((2, bsz, t, HEAD_W), F32),
        scratch_shapes=[pltpu.VMEM((GLA_H, GLA_DV, GLA_DK), F32)],
        compiler_params=_cparams(("arbitrary", "arbitrary", "arbitrary")),
        name="gla_scan",
    )(u, wdec, bdec, tri)


def _shifted(x, prev_row, next_row):
    n = x.shape[0]
    row = lax.broadcasted_iota(jnp.int32, x.shape, 0)
    xm1 = jnp.where(row == 0, prev_row, pltpu.roll(x, 1, 0))
    xp1 = jnp.where(row == n - 1, next_row, pltpu.roll(x, n - 1, 0))
    return xm1, xp1


def _halo_specs(width, blk_fn):
    per = BLK // 8

    def prev_map(*idx):
        b, blk = blk_fn(*idx)
        return (b, jnp.maximum(blk * per - 1, 0), 0)

    def next_map(nrow8):
        def f(*idx):
            b, blk = blk_fn(*idx)
            return (b, jnp.minimum((blk + 1) * per, nrow8 - 1), 0)
        return f

    return (lambda: pl.BlockSpec((1, 8, width), prev_map),
            lambda nrow8: pl.BlockSpec((1, 8, width), next_map(nrow8)))


def _stream_edges(blk, nbc, nb):
    has_prev = jnp.logical_and(blk != 0, blk != nbc)
    has_next = jnp.logical_and(blk != nbc - 1, blk != nb - 1)
    return has_prev.astype(F32), has_next.astype(F32)


def _ssd_kernel(u_ref, up_ref, un_ref, cw_ref, cb_ref, dtb_ref, aneg_ref, exp_ref, tri_ref,
                y_ref, xs_ref, h_ref, xdt_s, bm_s, cm_s, da_s, *, nbc, nb):
    d = pl.program_id(1)
    i = pl.program_id(2)
    blk = _block_of_step(d, i, nbc, nb)

    @pl.when(i == 0)
    def _():
        h_ref[...] = jnp.zeros_like(h_ref)

    has_prev, has_next = _stream_edges(blk, nbc, nb)
    u = u_ref[0]
    xbc = u[:, 256:768]
    prev_row = up_ref[0, 7:8, 256:768] * has_prev
    next_row = un_ref[0, 0:1, 256:768] * has_next
    xm1, xp1 = _shifted(xbc, prev_row, next_row)
    cw = cw_ref[...]
    act = _silu(cw[0:1] * xm1 + cw[1:2] * xbc + cw[2:3] * xp1 + cb_ref[...])
    xs = act[:, 0:256]
    xs_ref[0, 0] = xs
    dt = _softplus(_mm_mask_rhs(u[:, 768:896], exp_ref[...].astype(BF16)) + dtb_ref[0])
    xdt_s[...] = xs * dt
    da_s[...] = dt * aneg_ref[0]
    bm_s[...] = act[:, 256:384]
    cm_s[...] = act[:, 384:512]

    tri = tri_ref[d]
    tri_b = tri.astype(BF16)
    tri_o = tri_ref[1 - d].astype(BF16)

    def sub(j, carry):
        jj = jnp.where(d == 0, j, NSUB - 1 - j)
        r0 = pl.multiple_of(jj * CHUNK, CHUNK)
        rows = pl.ds(r0, CHUNK)
        da = da_s[rows, :]
        xdt = xdt_s[rows, :]
        bm = bm_s[rows, :]
        cm = cm_s[rows, :]
        cum = _mm_mask_lhs(tri_b, da)
        tot = jnp.sum(da, axis=0, keepdims=True)
        for g in range(SSM_G):
            gs = slice(SSM_N * g, SSM_N * (g + 1))
            bg = bm[:, gs]
            cg = cm[:, gs]
            cb = _dg(cg.astype(BF16), bg.astype(BF16), _NT)
            for r in range(SSM_H // SSM_G):
                h = g * (SSM_H // SSM_G) + r
                hs = slice(SSM_P * h, SSM_P * (h + 1))
                cum_col = cum[:, hs]
                cum_row = _mm_mask_rhs(da[:, hs], tri_o, _TN)
                seg = jnp.where(tri > 0, jnp.exp(cum_col - cum_row), 0.0)
                xh = xdt[:, hs].astype(BF16)
                hp = h_ref[h]
                y = _mm(cb * seg, xh) + _mm(cg * jnp.exp(cum_col), hp)
                y_ref[0, 0, rows, hs] = y
                to_end = jnp.exp(tot[:, hs] - cum_col)
                h_ref[h] = hp * jnp.exp(tot[:, hs]) + _dg((bg * to_end).astype(BF16), xh, _TN)
        return carry

    lax.fori_loop(0, NSUB, sub, 0)


def _ssd_scan(u, conv_w, conv_b, dtb, aneg, expand, tri, nbc):
    bsz, t, _ = u.shape
    nb = t // BLK
    blk = lambda b, d, i: _block_of_step(d, i, nbc, nb)
    prev_spec, next_spec = _halo_specs(SSM_W, lambda b, d, i: (b, blk(b, d, i)))
    full = lambda shape: pl.BlockSpec(shape, lambda b, d, i: (0,) * len(shape))
    return pl.pallas_call(
        functools.partial(_ssd_kernel, nbc=nbc, nb=nb),
        grid=(bsz, 2, nb),
        in_specs=[pl.BlockSpec((1, BLK, SSM_W), lambda b, d, i: (b, blk(b, d, i), 0)),
                  prev_spec(), next_spec(t // 8),
                  full((3, 512)), full((1, 512)),
                  pl.BlockSpec((1, 1, HEAD_W), lambda b, d, i: (d, 0, 0)),
                  pl.BlockSpec((1, 1, HEAD_W), lambda b, d, i: (d, 0, 0)),
                  full((LANE, HEAD_W)), full((2, CHUNK, CHUNK))],
        out_specs=[pl.BlockSpec((1, 1, BLK, HEAD_W), lambda b, d, i: (d, b, blk(b, d, i), 0))] * 2,
        out_shape=[jax.ShapeDtypeStruct((2, bsz, t, HEAD_W), F32)] * 2,
        scratch_shapes=[pltpu.VMEM((SSM_H, SSM_N, SSM_P), F32),
                        pltpu.VMEM((BLK, HEAD_W), F32), pltpu.VMEM((BLK, LANE), F32),
                        pltpu.VMEM((BLK, LANE), F32), pltpu.VMEM((BLK, HEAD_W), F32)],
        compiler_params=_cparams(("arbitrary", "arbitrary", "arbitrary")),
        name="ssd_scan",
    )(u, u, u, conv_w, conv_b, dtb, aneg, expand, tri)


def _rwkv_prep_kernel(u_ref, up_ref, un_ref, mu_ref, a0_ref, wa_ref, wg_ref, kk_ref, ka_ref, bd_ref,
                      r_ref, k_ref, v_ref, g_ref, kkn_ref, ab_ref, tlw_ref, *, nbc, nb):
    blk = pl.program_id(1)
    has_prev, has_next = _stream_edges(blk, nbc, nb)
    u = u_ref[0]
    xm1, xp1 = _shifted(u, up_ref[0, 7:8, :] * has_prev, un_ref[0, 0:1, :] * has_next)
    mu = mu_ref[...]
    u = u + mu[0:1] * (xm1 - u) + mu[1:2] * (xp1 - u)
    r = u[:, 0:256]
    k = u[:, 256:512]
    lwa = u[:, 768:896]
    a = _sigmoid(a0_ref[...] + _mm(lwa, wa_ref[...]))
    g = _mm(_sigmoid(u[:, 896:1024]), wg_ref[...])
    kk = k * kk_ref[...]
    ss = _mm_mask_rhs(kk * kk, bd_ref[...].astype(BF16))
    kk = kk / jnp.maximum(jnp.sqrt(ss), 1e-12)
    r_ref[0] = r
    k_ref[0] = k * (1.0 + (a - 1.0) * ka_ref[...])
    v_ref[0] = u[:, 512:768]
    g_ref[0] = g
    kkn_ref[0] = kk
    ab_ref[0] = kk * a
    tlw_ref[0] = jnp.tanh(lwa)


def _rwkv_prep(u, mu, a0, wa, wg, k_k, k_a, bd64, nbc):
    bsz, t, _ = u.shape
    nb = t // BLK
    prev_spec, next_spec = _halo_specs(RWKV_W, lambda b, i: (b, i))
    full = lambda shape: pl.BlockSpec(shape, lambda b, i: (0,) * len(shape))
    tok = lambda w_: pl.BlockSpec((1, BLK, w_), lambda b, i: (b, i, 0))
    return pl.pallas_call(
        functools.partial(_rwkv_prep_kernel, nbc=nbc, nb=nb),
        grid=(bsz, nb),
        in_specs=[tok(RWKV_W), prev_spec(), next_spec(t // 8),
                  full((2, RWKV_W)), full((1, HEAD_W)), full((LANE, HEAD_W)), full((LANE, HEAD_W)),
                  full((1, HEAD_W)), full((1, HEAD_W)), full((HEAD_W, HEAD_W))],
        out_specs=[tok(HEAD_W)] * 6 + [tok(LANE)],
        out_shape=[jax.ShapeDtypeStruct((bsz, t, HEAD_W), F32)] * 6
        + [jax.ShapeDtypeStruct((bsz, t, LANE), F32)],
        compiler_params=_cparams(("arbitrary", "arbitrary")),
        name="rwkv_prep",
    )(u, u, u, mu, a0, wa, wg, k_k, k_a, bd64)


def _rwkv_kernel(r_ref, k_ref, v_ref, kk_ref, ab_ref, tlw_ref, w0_ref, wdec_ref, tri_ref, stri_ref,
                 bd16_ref, eye_ref, y_ref, h_ref):
    d = pl.program_id(1)
    i = pl.program_id(2)

    @pl.when(i == 0)
    def _():
        h_ref[...] = jnp.zeros_like(h_ref)

    incl = tri_ref[d]
    strict = stri_ref[d]
    incl_b = incl.astype(BF16)
    bd16 = bd16_ref[...]
    eye = eye_ref[...]
    w0 = w0_ref[0]
    wdec = wdec_ref[0].astype(BF16)

    def sub(j, carry):
        jj = jnp.where(d == 0, j, NSUB - 1 - j)
        r0 = pl.multiple_of(jj * CHUNK, CHUNK)
        rows = pl.ds(r0, CHUNK)
        wr = w0 + _mm(tlw_ref[0, rows, :], wdec)
        lw = -jnp.exp(-_softplus(-wr) - 0.5)
        gc = _mm_mask_lhs(incl_b, lw)
        tot = jnp.sum(lw, axis=0, keepdims=True)
        eng = jnp.exp(-gc)
        e_end = jnp.exp(tot - gc)
        kkv = kk_ref[0, rows, :]
        abv = ab_ref[0, rows, :]
        k2 = k_ref[0, rows, :]
        vv = v_ref[0, rows, :]
        at = -kkv * jnp.exp(gc - lw)
        bt = abv * eng
        kt = k2 * eng
        rt = r_ref[0, rows, :] * jnp.exp(gc)
        bh = abv * e_end
        kh = k2 * e_end
        pc = jnp.exp(tot)
        for h in range(RWKV_H):
            hs = slice(RWKV_N * h, RWKV_N * (h + 1))
            ar = jnp.concatenate([at[:, hs], rt[:, hs]], axis=0)
            vh = vv[:, hs]
            g1 = _mm3(ar, bt[:, hs], _NT)
            g2 = _mm3(ar, kt[:, hs], _NT)
            a_ab = g1[:CHUNK] * strict
            a_rb = g1[CHUNK:] * incl
            a_ak = g2[:CHUNK] * strict
            a_rk = g2[CHUNK:] * incl
            ad = a_ab * bd16
            ee = a_ab - ad
            p = eye + ad
            a2 = _mm3(ad, ad)
            p = p + _mm3(p, a2)
            a4 = _mm3(a2, a2)
            p = p + _mm3(p, a4)
            a8 = _mm3(a4, a4)
            p = p + _mm3(p, a8)
            f = _mm3(p, ee)
            f2 = _mm3(f, f)
            t1 = p + _mm3(f, p)
            tinv = t1 + _mm3(f2, t1)
            ht = h_ref[h]
            ah = _mm3(ar, ht, _NT)
            uu = _mm3(tinv, ah[:CHUNK] + _mm3(a_ak, vh))
            y = ah[CHUNK:] + _mm3(a_rb, uu) + _mm3(a_rk, vh)
            y_ref[0, 0, rows, hs] = y
            h_ref[h] = ht * pc[:, hs] + _mm3(uu, bh[:, hs], _TN) + _mm3(vh, kh[:, hs], _TN)
        return carry

    lax.fori_loop(0, NSUB, sub, 0)


def _rwkv_scan(r, k2, v, kk, ab, tlw, w0, wdec, tri, stri, bd16, eye, nbc):
    bsz, t, _ = r.shape
    nb = t // BLK
    blk = lambda b, d, i: _block_of_step(d, i, nbc, nb)
    tok = lambda w_: pl.BlockSpec((1, BLK, w_), lambda b, d, i: (b, blk(b, d, i), 0))
    full = lambda shape: pl.BlockSpec(shape, lambda b, d, i: (0,) * len(shape))
    return pl.pallas_call(
        _rwkv_kernel,
        grid=(bsz, 2, nb),
        in_specs=[tok(HEAD_W)] * 5 + [tok(LANE),
                  pl.BlockSpec((1, 1, HEAD_W), lambda b, d, i: (d, 0, 0)),
                  pl.BlockSpec((1, LANE, HEAD_W), lambda b, d, i: (d, 0, 0)),
                  full((2, CHUNK, CHUNK)), full((2, CHUNK, CHUNK)),
                  full((CHUNK, CHUNK)), full((CHUNK, CHUNK))],
        out_specs=pl.BlockSpec((1, 1, BLK, HEAD_W), lambda b, d, i: (d, b, blk(b, d, i), 0)),
        out_shape=jax.ShapeDtypeStruct((2, bsz, t, HEAD_W), F32),
        scratch_shapes=[pltpu.VMEM((RWKV_H, RWKV_N, RWKV_N), F32)],
        compiler_params=_cparams(("arbitrary", "arbitrary", "arbitrary")),
        name="rwkv_scan",
    )(r, k2, v, kk, ab, tlw, w0, wdec, tri, stri, bd16, eye)


def _rope(x, c, sa, sb):
    w = x.shape[1]
    return x * c + pltpu.roll(x, w - ROPE_AXIS_DIM // 2, 1) * sa + pltpu.roll(x, ROPE_AXIS_DIM // 2, 1) * sb


def _att_prep_kernel(u_ref, c_ref, sa_ref, sb_ref, qg_ref, kg_ref, bd_ref, q_ref, kt_ref, v_ref):
    u = u_ref[0]
    bd = bd_ref[...].astype(BF16)
    c, sa, sb = c_ref[...], sa_ref[...], sb_ref[...]
    c2 = jnp.concatenate([c, c], axis=1)
    sa2 = jnp.concatenate([sa, sa], axis=1)
    sb2 = jnp.concatenate([sb, sb], axis=1)
    q = u[:, 0:256]
    k = u[:, 256:384]
    qms = _mm_mask_rhs(q * q, bd) * (1.0 / ATT_HD)
    kms = _mm_mask_rhs(k * k, bd[:LANE, :LANE]) * (1.0 / ATT_HD)
    qn = q * lax.rsqrt(qms + EPS) * qg_ref[...]
    kn = k * lax.rsqrt(kms + EPS) * kg_ref[...]
    q_ref[0] = _rope(qn, c2, sa2, sb2) * (ATT_HD ** -0.5)
    knt = _rope(kn, c, sa, sb).T
    kt_ref[0, 0] = knt[:ATT_HD].astype(BF16)
    kt_ref[0, 1] = knt[ATT_HD:].astype(BF16)
    v_ref[0, 0] = u[:, 384:448].astype(BF16)
    v_ref[0, 1] = u[:, 448:512].astype(BF16)


def _att_prep(u, c, sa, sb, qg, kg, bd64):
    bsz, t, _ = u.shape
    nb = t // BLK
    full = lambda shape: pl.BlockSpec(shape, lambda b, i: (0,) * len(shape))
    tab = pl.BlockSpec((BLK, LANE), lambda b, i: (i, 0))
    return pl.pallas_call(
        _att_prep_kernel,
        grid=(bsz, nb),
        in_specs=[pl.BlockSpec((1, BLK, ATT_W), lambda b, i: (b, i, 0)), tab, tab, tab,
                  full((1, HEAD_W)), full((1, LANE)), full((HEAD_W, HEAD_W))],
        out_specs=[pl.BlockSpec((1, BLK, HEAD_W), lambda b, i: (b, i, 0)),
                   pl.BlockSpec((1, ATT_HKV, ATT_HD, BLK), lambda b, i: (b, 0, 0, i)),
                   pl.BlockSpec((1, ATT_HKV, BLK, ATT_HD), lambda b, i: (b, 0, i, 0))],
        out_shape=[jax.ShapeDtypeStruct((bsz, t, HEAD_W), F32),
                   jax.ShapeDtypeStruct((bsz, ATT_HKV, ATT_HD, t), BF16),
                   jax.ShapeDtypeStruct((bsz, ATT_HKV, t, ATT_HD), BF16)],
        compiler_params=_cparams(("arbitrary", "arbitrary")),
        name="att_prep",
    )(u, c, sa, sb, qg, kg, bd64)


def _att_kernel(q_ref, kt_ref, v_ref, o_ref, *, nbc, n_ctx):
    i = pl.program_id(2)
    q = q_ref[0]
    qs = jnp.concatenate([q[:, :ATT_HD], q[:, ATT_HD:]], axis=0).astype(BF16)
    s = jnp.dot(qs, kt_ref[0, 0], preferred_element_type=F32)
    t = s.shape[1]
    col = lax.broadcasted_iota(jnp.int32, s.shape, 1)
    limit = jnp.where(i < nbc, n_ctx, t)
    s = jnp.where(col < limit, s, -1e30)
    e = jnp.exp(s - jnp.max(s, axis=-1, keepdims=True))
    l = jnp.sum(e, axis=-1, keepdims=True)
    o = jnp.dot(e.astype(BF16), v_ref[0, 0], preferred_element_type=F32) / l
    o_ref[0, :, 0:ATT_HD] = o[:BLK]
    o_ref[0, :, ATT_HD:2 * ATT_HD] = o[BLK:]


def _attention(q, kt, v, nbc, n_ctx):
    bsz, t, _ = q.shape
    nb = t // BLK
    return pl.pallas_call(
        functools.partial(_att_kernel, nbc=nbc, n_ctx=n_ctx),
        grid=(bsz, ATT_HKV, nb),
        in_specs=[pl.BlockSpec((1, BLK, LANE), lambda b, g, i: (b, i, g)),
                  pl.BlockSpec((1, 1, ATT_HD, t), lambda b, g, i: (b, g, 0, 0)),
                  pl.BlockSpec((1, 1, t, ATT_HD), lambda b, g, i: (b, g, 0, 0))],
        out_specs=pl.BlockSpec((1, BLK, LANE), lambda b, g, i: (b, i, g)),
        out_shape=jax.ShapeDtypeStruct((bsz, t, HEAD_W), F32),
        compiler_params=_cparams(("arbitrary", "arbitrary", "arbitrary")),
        name="attention",
    )(q, kt, v)


def _out_kernel(h_ref, mod_ref, glaf_ref, glab_ref, glag_ref, ssdf_ref, ssdb_ref, xs_ref, z_ref,
                rwf_ref, rwb_ref, r_ref, k_ref, v_ref, g_ref, att_ref,
                glan_ref, ssd_d_ref, ssdn_ref, rk_ref, lng_ref, lnb_ref, bd_ref, w_ref, o_ref):
    bd = bd_ref[...].astype(BF16)
    seg_mean = lambda x: _mm_mask_rhs(x, bd) * (1.0 / 64.0)
    o = glaf_ref[0, 0] + glab_ref[0, 0]
    y_gla = o * lax.rsqrt(seg_mean(o * o) + EPS) * glan_ref[...] * _silu(glag_ref[0])
    y = ssdf_ref[0, 0] + ssdb_ref[0, 0] + ssd_d_ref[...] * xs_ref[0, 0]
    y = y * _silu(z_ref[0])
    y_ssd = y * lax.rsqrt(jnp.mean(y * y, axis=-1, keepdims=True) + EPS) * ssdn_ref[...]
    y = rwf_ref[0, 0] + rwb_ref[0, 0]
    mu = seg_mean(y)
    yc = y - mu
    var = seg_mean(yc * yc)
    yn = yc * lax.rsqrt(var + RWKV_GN_EPS) * lng_ref[...] + lnb_ref[...]
    v = v_ref[0]
    bonus = _mm_mask_rhs(r_ref[0] * k_ref[0] * rk_ref[...], bd) * v
    y_rw = (yn + bonus) * g_ref[0]
    w = w_ref[...]
    proj = (_mm(y_gla, w[0:256]) + _mm(y_ssd, w[256:512])) + (_mm(y_rw, w[512:768]) + _mm(att_ref[0], w[768:1024]))
    o_ref[0] = h_ref[0] + mod_ref[0][5:6] * proj


def _mix_out(h, mod, gla_o, u_gla, ssd_y, ssd_xs, u_ssm, rw_y, r, k2, v, g, att_o,
             gla_n, ssd_d, ssd_n, r_k, ln_g, ln_b, bd64, w_out, nbc):
    bsz, t, d = h.shape
    tok = lambda w_, col=0: pl.BlockSpec((1, BLK, w_), lambda b, i: (b, i, col))
    dirn = lambda dd: pl.BlockSpec((1, 1, BLK, HEAD_W), lambda b, i: (dd, b, i, 0))
    full = lambda shape: pl.BlockSpec(shape, lambda b, i: (0,) * len(shape))
    vec = full((1, HEAD_W))
    return pl.pallas_call(
        _out_kernel,
        grid=(bsz, t // BLK),
        in_specs=[tok(d), pl.BlockSpec((1, N_MOD, d), lambda b, i: (_mod_row(b, i, nbc), 0, 0)),
                  dirn(0), dirn(1), tok(HEAD_W, 2),
                  dirn(0), dirn(1), dirn(0), tok(HEAD_W, 0),
                  dirn(0), dirn(1), tok(HEAD_W), tok(HEAD_W), tok(HEAD_W), tok(HEAD_W), tok(HEAD_W),
                  vec, vec, vec, vec, vec, vec, full((HEAD_W, HEAD_W)), full((d, d))],
        out_specs=tok(d),
        out_shape=jax.ShapeDtypeStruct(h.shape, F32),
        compiler_params=_cparams(("arbitrary", "arbitrary")),
        name="mix_out",
    )(h, mod, gla_o, gla_o, u_gla, ssd_y, ssd_y, ssd_xs, u_ssm, rw_y, rw_y, r, k2, v, g, att_o,
      gla_n, ssd_d, ssd_n, r_k, ln_g, ln_b, bd64, w_out)


def _rope_tables(n_ctx, n_lat):
    rows = n_lat // GRID_W
    row = jnp.repeat(jnp.arange(rows, dtype=F32), GRID_W)
    col = jnp.tile(jnp.arange(GRID_W, dtype=F32), rows)
    inv = ROPE_THETA ** (-jnp.arange(0, ROPE_AXIS_DIM, 2, dtype=F32) / ROPE_AXIS_DIM)
    ang = jnp.stack([row[:, None] * inv, col[:, None] * inv], axis=1)
    cos, sin = jnp.cos(ang), jnp.sin(ang)
    zero = jnp.zeros_like(sin)
    c = jnp.concatenate([cos, cos], axis=-1).reshape(n_lat, ATT_HD)
    sa = jnp.concatenate([-sin, zero], axis=-1).reshape(n_lat, ATT_HD)
    sb = jnp.concatenate([zero, sin], axis=-1).reshape(n_lat, ATT_HD)
    pad = lambda x, fill: jnp.concatenate([jnp.full((n_ctx, ATT_HD), fill, F32), x], axis=0)
    two = lambda x: jnp.concatenate([x, x], axis=1)
    return two(pad(c, 1.0)), two(pad(sa, 0.0)), two(pad(sb, 0.0))


def _masks():
    t = jnp.arange(CHUNK)
    lower = (t[None, :] <= t[:, None]).astype(F32)
    slower = (t[None, :] < t[:, None]).astype(F32)
    tri = jnp.stack([lower, lower.T])
    stri = jnp.stack([slower, slower.T])
    bd16 = (t[None, :] // SUB == t[:, None] // SUB).astype(F32)
    eye = jnp.eye(CHUNK, dtype=F32)
    c = jnp.arange(HEAD_W)
    bd64 = (c[None, :] // 64 == c[:, None] // 64).astype(F32)
    expand = (jnp.arange(LANE)[:, None] == c[None, :] // 64).astype(F32)
    return tri, stri, bd16, eye, bd64, expand


def _pad_cols(x, width):
    return jnp.pad(x, [(0, 0)] * (x.ndim - 1) + [(0, width - x.shape[-1])])


def _pad_rows(x, height, top=0):
    return jnp.pad(x, [(0, 0)] * (x.ndim - 2) + [(top, height - top - x.shape[-2]), (0, 0)])


def _rep(x, n):
    return jnp.repeat(x, n, axis=-1)


def kernel(x, c, ctx, c_ctx, norm_g, w_mod, b_mod, ffn_in, ffn_out, w_in, w_out, gla_w_dec, gla_b_dec, gla_norm, ssm_conv_w, ssm_conv_b, ssm_dt_bias, ssm_a_log, ssm_d, ssm_norm, rwkv_shift_mu, rwkv_w0, rwkv_w_dec, rwkv_a0, rwkv_w_a, rwkv_w_g, rwkv_k_k, rwkv_k_a, rwkv_r_k, rwkv_ln_g, rwkv_ln_b, att_q_norm, att_k_norm):
    bsz, n_lat, d = x.shape
    n_ctx = ctx.shape[1]
    depth = w_mod.shape[0]
    assert bsz == 4 and n_ctx % BLK == 0 and n_lat % BLK == 0 and n_lat % GRID_W == 0
    nbc = n_ctx // BLK
    tri, stri, bd16, eye, bd64, expand = _masks()
    rope_c, rope_sa, rope_sb = _rope_tables(n_ctx, n_lat)

    cvec = jnp.concatenate([c, c_ctx[None], jnp.zeros((8 - bsz - 1, d), F32)], axis=0)
    mod_all = _compute_mod(cvec, w_mod, b_mod).reshape(depth, 8, N_MOD, d)

    h = jnp.concatenate([ctx, x], axis=1)
    for l in range(depth):
        mod = mod_all[l]
        w1 = ffn_in[l].astype(BF16)
        w2 = ffn_out[l].astype(BF16)
        wi = w_in[l]
        o_ssm = 784
        o_rw = o_ssm + 772
        o_att = o_rw + 1024
        w_proj = jnp.concatenate([_pad_cols(wi[:, :o_ssm], GLA_W), _pad_cols(wi[:, o_ssm:o_rw], SSM_W),
                                  wi[:, o_rw:o_att], wi[:, o_att:]], axis=1).astype(BF16)
        gla_wd = _pad_rows(gla_w_dec[l], LANE)
        gla_bd = gla_b_dec[l][:, None, :]
        ssm_dtb = _rep(ssm_dt_bias[l], SSM_P)[:, None, :]
        ssm_an = _rep(-jnp.exp(ssm_a_log[l]), SSM_P)[:, None, :]
        rw_wd = _pad_rows(rwkv_w_dec[l], LANE)
        rw_wa = _pad_rows(rwkv_w_a[l], LANE, top=64)
        rw_w0 = rwkv_w0[l][:, None, :]

        h = _ffn(h, mod, norm_g[l, 0][None], w1[0], w2[0], 0, nbc)
        u_gla, u_ssm, u_rw, u_att = _inproj(h, mod, norm_g[l, 1][None], w_proj, nbc)

        gla_o = _gla_scan(u_gla, gla_wd, gla_bd, tri, nbc)
        ssd_y, ssd_xs = _ssd_scan(u_ssm, ssm_conv_w[l], ssm_conv_b[l][None], ssm_dtb, ssm_an, expand, tri, nbc)
        r, k2, v, g, kk, ab, tlw = _rwkv_prep(u_rw, rwkv_shift_mu[l], rwkv_a0[l][None], rw_wa, rwkv_w_g[l],
                                              rwkv_k_k[l][None], rwkv_k_a[l][None], bd64, nbc)
        rw_y = _rwkv_scan(r, k2, v, kk, ab, tlw, rw_w0, rw_wd, tri, stri, bd16, eye, nbc)
        qn, kt, vt = _att_prep(u_att, rope_c, rope_sa, rope_sb, jnp.tile(att_q_norm[l], ATT_HQ)[None],
                               jnp.tile(att_k_norm[l], ATT_HKV)[None], bd64)
        att_o = _attention(qn, kt, vt, nbc, n_ctx)

        h = _mix_out(h, mod, gla_o, u_gla, ssd_y, ssd_xs, u_ssm, rw_y, r, k2, v, g, att_o,
                     jnp.tile(gla_norm[l], GLA_H)[None], _rep(ssm_d[l], SSM_P)[None], ssm_norm[l][None],
                     rwkv_r_k[l].reshape(1, HEAD_W), rwkv_ln_g[l][None], rwkv_ln_b[l][None], bd64,
                     w_out[l].astype(BF16), nbc)
        h = _ffn(h, mod, norm_g[l, 2][None], w1[1], w2[1], 6, nbc)
    return h[:, n_ctx:]
```

```python
import functools
import math

import jax
import jax.numpy as jnp
from jax import lax
from jax.experimental import pallas as pl
from jax.experimental.pallas import tpu as pltpu

F32 = jnp.float32
BF16 = jnp.bfloat16

N_MOD = 9
EPS = 1e-6
GLA_H, GLA_DK, GLA_DV, GLA_LORA, GLA_TAU = 4, 32, 64, 16, 16.0
SSM_H, SSM_P, SSM_G, SSM_N = 4, 64, 2, 64
RWKV_H, RWKV_N, RWKV_GN_EPS = 4, 64, 64e-5
ATT_HQ, ATT_HKV, ATT_HD = 4, 2, 64
GRID_W = 64
ROPE_THETA = 10000.0
ROPE_AXIS_DIM = ATT_HD // 2

CHUNK = 64
SUB = 16
BLK = 256
NSUB = BLK // CHUNK
LANE = 128
HEAD_W = 256
VMEM_LIMIT = 56 * 1024 * 1024

GLA_W = 896
SSM_W = 896
RWKV_W = 1024
ATT_W = 512


def _cparams(sem):
    return pltpu.CompilerParams(dimension_semantics=sem, vmem_limit_bytes=VMEM_LIMIT)


def _mm(a, b):
    return jnp.dot(a.astype(BF16), b.astype(BF16), preferred_element_type=F32)


def _dg(a, b, dims):
    return lax.dot_general(a, b, (dims, ((), ())), preferred_element_type=F32)


_NN = ((1,), (0,))
_NT = ((1,), (1,))
_TN = ((0,), (0,))


def _split2(x):
    hi = x.astype(BF16)
    lo = (x - hi.astype(F32)).astype(BF16)
    return hi, lo


def _split3(x):
    hi = x.astype(BF16)
    r1 = x - hi.astype(F32)
    mid = r1.astype(BF16)
    lo = (r1 - mid.astype(F32)).astype(BF16)
    return hi, mid, lo


def _mm3(a, b, dims=_NN):
    ah, al = _split2(a)
    bh, bl = _split2(b)
    return _dg(ah, bh, dims) + (_dg(ah, bl, dims) + _dg(al, bh, dims))


def _mm3s(a, b, dims=_NN):
    return _dg(a[0], b[0], dims) + (_dg(a[0], b[1], dims) + _dg(a[1], b[0], dims))


def _mm_mask_lhs(mask_bf16, x, dims=_NN):
    hi, mid, lo = _split3(x)
    return _dg(mask_bf16, hi, dims) + (_dg(mask_bf16, mid, dims) + _dg(mask_bf16, lo, dims))


def _mm_mask_rhs(x, mask_bf16, dims=_NN):
    hi, mid, lo = _split3(x)
    return _dg(hi, mask_bf16, dims) + (_dg(mid, mask_bf16, dims) + _dg(lo, mask_bf16, dims))


def _sigmoid(x):
    return jax.nn.sigmoid(x)


def _silu(x):
    return x * jax.nn.sigmoid(x)


def _softplus(x):
    return jnp.maximum(x, 0.0) + jnp.log1p(jnp.exp(-jnp.abs(x)))


def _modulate(x, g, shift, scale):
    ms = jnp.mean(x * x, axis=-1, keepdims=True)
    return (x * lax.rsqrt(ms + EPS) * g) * (1.0 + scale) + shift


def _block_of_step(d, i, nbc, nb):
    back = jnp.where(i < nbc, nbc - 1 - i, nb + nbc - 1 - i)
    return jnp.where(d == 0, i, back)


def _mod_row(b, i, nbc):
    return jnp.where(i < nbc, 4, b)


def _mod_kernel(c_ref, w_ref, b_ref, o_ref):
    o_ref[0] = _mm(_silu(c_ref[...]), w_ref[0]) + b_ref[0]


def _compute_mod(cvec, w_mod, b_mod):
    depth, d, n = w_mod.shape
    tn = 1024
    return pl.pallas_call(
        _mod_kernel,
        grid=(depth, n // tn),
        in_specs=[pl.BlockSpec((8, d), lambda l, j: (0, 0)),
                  pl.BlockSpec((1, d, tn), lambda l, j: (l, 0, j)),
                  pl.BlockSpec((1, 1, tn), lambda l, j: (l, 0, j))],
        out_specs=pl.BlockSpec((1, 8, tn), lambda l, j: (l, 0, j)),
        out_shape=jax.ShapeDtypeStruct((depth, 8, n), F32),
        compiler_params=_cparams(("arbitrary", "arbitrary")),
        name="adaln_mod",
    )(cvec, w_mod, b_mod.reshape(depth, 1, n))


def _ffn_kernel(h_ref, mod_ref, g_ref, w1_ref, w2_ref, o_ref, *, i0, d_ff):
    x = h_ref[0]
    m = mod_ref[0]
    hm = _modulate(x, g_ref[...], m[i0:i0 + 1], m[i0 + 1:i0 + 2]).astype(BF16)
    ab = jnp.dot(hm, w1_ref[...], preferred_element_type=F32)
    a = ab[:, :d_ff]
    act = (_silu(a) * ab[:, d_ff:]).astype(BF16)
    y = jnp.dot(act, w2_ref[...], preferred_element_type=F32)
    o_ref[0] = x + (0.5 * m[i0 + 2:i0 + 3]) * y


def _ffn(h, mod, g, w1, w2, i0, nbc):
    bsz, t, d = h.shape
    d_ff = w2.shape[0]
    return pl.pallas_call(
        functools.partial(_ffn_kernel, i0=i0, d_ff=d_ff),
        grid=(bsz, t // BLK),
        in_specs=[pl.BlockSpec((1, BLK, d), lambda b, i: (b, i, 0)),
                  pl.BlockSpec((1, N_MOD, d), lambda b, i: (_mod_row(b, i, nbc), 0, 0)),
                  pl.BlockSpec((1, d), lambda b, i: (0, 0)),
                  pl.BlockSpec((d, 2 * d_ff), lambda b, i: (0, 0)),
                  pl.BlockSpec((d_ff, d), lambda b, i: (0, 0))],
        out_specs=pl.BlockSpec((1, BLK, d), lambda b, i: (b, i, 0)),
        out_shape=jax.ShapeDtypeStruct(h.shape, F32),
        compiler_params=_cparams(("arbitrary", "arbitrary")),
        name="ffn",
    )(h, mod, g, w1, w2)


def _inproj_kernel(h_ref, mod_ref, g_ref, w_ref, gla_ref, ssm_ref, rwkv_ref, att_ref):
    m = mod_ref[0]
    hm = _modulate(h_ref[0], g_ref[...], m[3:4], m[4:5]).astype(BF16)
    u = jnp.dot(hm, w_ref[...], preferred_element_type=F32)
    gla_ref[0] = u[:, :GLA_W]
    ssm_ref[0] = u[:, GLA_W:GLA_W + SSM_W]
    rwkv_ref[0] = u[:, GLA_W + SSM_W:GLA_W + SSM_W + RWKV_W]
    att_ref[0] = u[:, GLA_W + SSM_W + RWKV_W:]


def _inproj(h, mod, g, w, nbc):
    bsz, t, d = h.shape
    widths = (GLA_W, SSM_W, RWKV_W, ATT_W)
    return pl.pallas_call(
        _inproj_kernel,
        grid=(bsz, t // BLK),
        in_specs=[pl.BlockSpec((1, BLK, d), lambda b, i: (b, i, 0)),
                  pl.BlockSpec((1, N_MOD, d), lambda b, i: (_mod_row(b, i, nbc), 0, 0)),
                  pl.BlockSpec((1, d), lambda b, i: (0, 0)),
                  pl.BlockSpec((d, sum(widths)), lambda b, i: (0, 0))],
        out_specs=[pl.BlockSpec((1, BLK, w_), lambda b, i: (b, i, 0)) for w_ in widths],
        out_shape=[jax.ShapeDtypeStruct((bsz, t, w_), F32) for w_ in widths],
        compiler_params=_cparams(("arbitrary", "arbitrary")),
        name="inproj",
    )(h, mod, g, w)


def _gla_kernel(u_ref, wdec_ref, bdec_ref, tri_ref, o_ref, s_ref):
    d = pl.program_id(1)
    i = pl.program_id(2)

    @pl.when(i == 0)
    def _():
        s_ref[...] = jnp.zeros_like(s_ref)

    tri = tri_ref[d]
    tri_b = tri.astype(BF16)
    wdec = wdec_ref[0].astype(BF16)
    bdec = bdec_ref[0]

    rows_of, qh, kih, keh, vh, dec = [], [], [], [], [], []
    for j in range(NSUB):
        jj = jnp.where(d == 0, j, NSUB - 1 - j)
        rows = pl.ds(pl.multiple_of(jj * CHUNK, CHUNK), CHUNK)
        rows_of.append(rows)
        u = u_ref[0, rows, :]
        q = u[:, 0:128] * (GLA_DK ** -0.5)
        k = u[:, 128:256]
        v = u[:, 256:512]
        z = _mm(u[:, 768:896], wdec) + bdec
        la = -_softplus(-z) * (1.0 / GLA_TAU)
        bc = _mm_mask_lhs(tri_b, la)
        bt = jnp.sum(la, axis=0, keepdims=True)
        qd = q * jnp.exp(bc)
        ki = k * jnp.exp(-bc)
        ke = k * jnp.exp(bt - bc)
        dec_j = jnp.exp(bt)
        for h in range(GLA_H):
            ks = slice(GLA_DK * h, GLA_DK * (h + 1))
            qh.append(qd[:, ks].astype(BF16))
            kih.append(ki[:, ks].astype(BF16))
            keh.append(ke[:, ks].astype(BF16))
            vh.append(v[:, GLA_DV * h:GLA_DV * (h + 1)].astype(BF16))
            dec.append(dec_j[:, ks])
    att = [_dg(a, b, _NT) * tri for a, b in zip(qh, kih)]
    o_intra = [_mm(a, b) for a, b in zip(att, vh)]
    kv = [_dg(a, b, _TN) for a, b in zip(vh, keh)]
    st = [s_ref[h] for h in range(GLA_H)]
    s_prev = []
    for j in range(NSUB):
        for h in range(GLA_H):
            n = j * GLA_H + h
            s_prev.append(st[h].astype(BF16))
            st[h] = st[h] * dec[n] + kv[n]
    for h in range(GLA_H):
        s_ref[h] = st[h]
    for j in range(NSUB):
        for h in range(GLA_H):
            n = j * GLA_H + h
            o_ref[0, 0, rows_of[j], GLA_DV * h:GLA_DV * (h + 1)] = o_intra[n] + _dg(qh[n], s_prev[n], _NT)


def _gla_scan(u, wdec, bdec, tri, nbc):
    bsz, t, _ = u.shape
    nb = t // BLK
    blk = lambda b, d, i: _block_of_step(d, i, nbc, nb)
    return pl.pallas_call(
        _gla_kernel,
        grid=(bsz, 2, nb),
        in_specs=[pl.BlockSpec((1, BLK, GLA_W), lambda b, d, i: (b, blk(b, d, i), 0)),
                  pl.BlockSpec((1, LANE, LANE), lambda b, d, i: (d, 0, 0)),
                  pl.BlockSpec((1, 1, LANE), lambda b, d, i: (d, 0, 0)),
                  pl.BlockSpec((2, CHUNK, CHUNK), lambda b, d, i: (0, 0, 0))],
        out_specs=pl.BlockSpec((1, 1, BLK, HEAD_W), lambda b, d, i: (d, b, blk(b, d, i), 0)),
        out_shape=jax.ShapeDtypeStruct((2, bsz, t, HEAD_W), F32),
        scratch_shapes=[pltpu.VMEM((GLA_H, GLA_DV, GLA_DK), F32)],
        compiler_params=_cparams(("arbitrary", "arbitrary", "arbitrary")),
        name="gla_scan",
    )(u, wdec, bdec, tri)


def _shifted(x, prev_row, next_row):
    n = x.shape[0]
    row = lax.broadcasted_iota(jnp.int32, x.shape, 0)
    xm1 = jnp.where(row == 0, prev_row, pltpu.roll(x, 1, 0))
    xp1 = jnp.where(row == n - 1, next_row, pltpu.roll(x, n - 1, 0))
    return xm1, xp1


def _halo_specs(width, blk_fn):
    per = BLK // 8

    def prev_map(*idx):
        b, blk = blk_fn(*idx)
        return (b, jnp.maximum(blk * per - 1, 0), 0)

    def next_map(nrow8):
        def f(*idx):
            b, blk = blk_fn(*idx)
            return (b, jnp.minimum((blk + 1) * per, nrow8 - 1), 0)
        return f

    return (lambda: pl.BlockSpec((1, 8, width), prev_map),
            lambda nrow8: pl.BlockSpec((1, 8, width), next_map(nrow8)))


def _stream_edges(blk, nbc, nb):
    has_prev = jnp.logical_and(blk != 0, blk != nbc)
    has_next = jnp.logical_and(blk != nbc - 1, blk != nb - 1)
    return has_prev.astype(F32), has_next.astype(F32)


def _ssd_kernel(u_ref, up_ref, un_ref, cw_ref, cb_ref, dtb_ref, aneg_ref, exp_ref, tri_ref,
                y_ref, xs_ref, h_ref, xdt_s, bm_s, cm_s, da_s, *, nbc, nb):
    d = pl.program_id(1)
    i = pl.program_id(2)
    blk = _block_of_step(d, i, nbc, nb)

    @pl.when(i == 0)
    def _():
        h_ref[...] = jnp.zeros_like(h_ref)

    has_prev, has_next = _stream_edges(blk, nbc, nb)
    u = u_ref[0]
    xbc = u[:, 256:768]
    prev_row = up_ref[0, 7:8, 256:768] * has_prev
    next_row = un_ref[0, 0:1, 256:768] * has_next
    xm1, xp1 = _shifted(xbc, prev_row, next_row)
    cw = cw_ref[...]
    act = _silu(cw[0:1] * xm1 + cw[1:2] * xbc + cw[2:3] * xp1 + cb_ref[...])
    xs = act[:, 0:256]
    xs_ref[0, 0] = xs
    dt = _softplus(_mm_mask_rhs(u[:, 768:896], exp_ref[...].astype(BF16)) + dtb_ref[0])
    xdt_s[...] = xs * dt
    da_s[...] = dt * aneg_ref[0]
    bm_s[...] = act[:, 256:384]
    cm_s[...] = act[:, 384:512]

    tri = tri_ref[d]
    tri_b = tri.astype(BF16)
    tri_o = tri_ref[1 - d].astype(BF16)

    per_g = SSM_H // SSM_G
    rows_of, cgs, bgs, das, xh, cum_col, dec = [], [], [], [], [], [], []
    for j in range(NSUB):
        jj = jnp.where(d == 0, j, NSUB - 1 - j)
        rows = pl.ds(pl.multiple_of(jj * CHUNK, CHUNK), CHUNK)
        rows_of.append(rows)
        da = da_s[rows, :]
        xdt = xdt_s[rows, :]
        bm = bm_s[rows, :]
        cm = cm_s[rows, :]
        cum = _mm_mask_lhs(tri_b, da)
        tot = jnp.sum(da, axis=0, keepdims=True)
        for h in range(SSM_H):
            hs = slice(SSM_P * h, SSM_P * (h + 1))
            gs = slice(SSM_N * (h // per_g), SSM_N * (h // per_g + 1))
            cgs.append(cm[:, gs])
            bgs.append(bm[:, gs])
            das.append(da[:, hs])
            xh.append(xdt[:, hs].astype(BF16))
            cum_col.append(cum[:, hs])
            dec.append(tot[:, hs])
    cb = [_dg(c.astype(BF16), b.astype(BF16), _NT) for c, b in zip(cgs[::per_g], bgs[::per_g])]
    cum_row = [_mm_mask_rhs(a, tri_o, _TN) for a in das]
    seg = [jnp.where(tri > 0, jnp.exp(c - r), 0.0) for c, r in zip(cum_col, cum_row)]
    y_intra = [_mm(cb[n // per_g] * s, x) for n, (s, x) in enumerate(zip(seg, xh))]
    contrib = [_dg((b * jnp.exp(t - c)).astype(BF16), x, _TN) for b, t, c, x in zip(bgs, dec, cum_col, xh)]
    hp = [h_ref[h] for h in range(SSM_H)]
    h_prev = []
    for j in range(NSUB):
        for h in range(SSM_H):
            n = j * SSM_H + h
            h_prev.append(hp[h])
            hp[h] = hp[h] * jnp.exp(dec[n]) + contrib[n]
    for h in range(SSM_H):
        h_ref[h] = hp[h]
    for j in range(NSUB):
        for h in range(SSM_H):
            n = j * SSM_H + h
            y_ref[0, 0, rows_of[j], SSM_P * h:SSM_P * (h + 1)] = (
                y_intra[n] + _mm(cgs[n] * jnp.exp(cum_col[n]), h_prev[n]))


def _ssd_scan(u, conv_w, conv_b, dtb, aneg, expand, tri, nbc):
    bsz, t, _ = u.shape
    nb = t // BLK
    blk = lambda b, d, i: _block_of_step(d, i, nbc, nb)
    prev_spec, next_spec = _halo_specs(SSM_W, lambda b, d, i: (b, blk(b, d, i)))
    full = lambda shape: pl.BlockSpec(shape, lambda b, d, i: (0,) * len(shape))
    return pl.pallas_call(
        functools.partial(_ssd_kernel, nbc=nbc, nb=nb),
        grid=(bsz, 2, nb),
        in_specs=[pl.BlockSpec((1, BLK, SSM_W), lambda b, d, i: (b, blk(b, d, i), 0)),
                  prev_spec(), next_spec(t // 8),
                  full((3, 512)), full((1, 512)),
                  pl.BlockSpec((1, 1, HEAD_W), lambda b, d, i: (d, 0, 0)),
                  pl.BlockSpec((1, 1, HEAD_W), lambda b, d, i: (d, 0, 0)),
                  full((LANE, HEAD_W)), full((2, CHUNK, CHUNK))],
        out_specs=[pl.BlockSpec((1, 1, BLK, HEAD_W), lambda b, d, i: (d, b, blk(b, d, i), 0))] * 2,
        out_shape=[jax.ShapeDtypeStruct((2, bsz, t, HEAD_W), F32)] * 2,
        scratch_shapes=[pltpu.VMEM((SSM_H, SSM_N, SSM_P), F32),
                        pltpu.VMEM((BLK, HEAD_W), F32), pltpu.VMEM((BLK, LANE), F32),
                        pltpu.VMEM((BLK, LANE), F32), pltpu.VMEM((BLK, HEAD_W), F32)],
        compiler_params=_cparams(("arbitrary", "arbitrary", "arbitrary")),
        name="ssd_scan",
    )(u, u, u, conv_w, conv_b, dtb, aneg, expand, tri)


def _rwkv_prep_kernel(u_ref, up_ref, un_ref, mu_ref, a0_ref, wa_ref, wg_ref, kk_ref, ka_ref, bd_ref,
                      r_ref, k_ref, v_ref, g_ref, kkn_ref, ab_ref, tlw_ref, *, nbc, nb):
    blk = pl.program_id(1)
    has_prev, has_next = _stream_edges(blk, nbc, nb)
    u = u_ref[0]
    xm1, xp1 = _shifted(u, up_ref[0, 7:8, :] * has_prev, un_ref[0, 0:1, :] * has_next)
    mu = mu_ref[...]
    u = u + mu[0:1] * (xm1 - u) + mu[1:2] * (xp1 - u)
    r = u[:, 0:256]
    k = u[:, 256:512]
    lwa = u[:, 768:896]
    a = _sigmoid(a0_ref[...] + _mm(lwa, wa_ref[...]))
    g = _mm(_sigmoid(u[:, 896:1024]), wg_ref[...])
    kk = k * kk_ref[...]
    ss = _mm_mask_rhs(kk * kk, bd_ref[...].astype(BF16))
    kk = kk / jnp.maximum(jnp.sqrt(ss), 1e-12)
    r_ref[0] = r
    k_ref[0] = k * (1.0 + (a - 1.0) * ka_ref[...])
    v_ref[0] = u[:, 512:768]
    g_ref[0] = g
    kkn_ref[0] = kk
    ab_ref[0] = kk * a
    tlw_ref[0] = jnp.tanh(lwa)


def _rwkv_prep(u, mu, a0, wa, wg, k_k, k_a, bd64, nbc):
    bsz, t, _ = u.shape
    nb = t // BLK
    prev_spec, next_spec = _halo_specs(RWKV_W, lambda b, i: (b, i))
    full = lambda shape: pl.BlockSpec(shape, lambda b, i: (0,) * len(shape))
    tok = lambda w_: pl.BlockSpec((1, BLK, w_), lambda b, i: (b, i, 0))
    return pl.pallas_call(
        functools.partial(_rwkv_prep_kernel, nbc=nbc, nb=nb),
        grid=(bsz, nb),
        in_specs=[tok(RWKV_W), prev_spec(), next_spec(t // 8),
                  full((2, RWKV_W)), full((1, HEAD_W)), full((LANE, HEAD_W)), full((LANE, HEAD_W)),
                  full((1, HEAD_W)), full((1, HEAD_W)), full((HEAD_W, HEAD_W))],
        out_specs=[tok(HEAD_W)] * 6 + [tok(LANE)],
        out_shape=[jax.ShapeDtypeStruct((bsz, t, HEAD_W), F32)] * 6
        + [jax.ShapeDtypeStruct((bsz, t, LANE), F32)],
        compiler_params=_cparams(("arbitrary", "arbitrary")),
        name="rwkv_prep",
    )(u, u, u, mu, a0, wa, wg, k_k, k_a, bd64)


def _rwkv_kernel(r_ref, k_ref, v_ref, kk_ref, ab_ref, tlw_ref, w0_ref, wdec_ref, tri_ref, stri_ref,
                 bd16_ref, eye_ref, y_ref, h_ref):
    d = pl.program_id(1)
    i = pl.program_id(2)

    @pl.when(i == 0)
    def _():
        h_ref[...] = jnp.zeros_like(h_ref)

    incl = tri_ref[d]
    strict = stri_ref[d]
    incl_b = incl.astype(BF16)
    bd16 = bd16_ref[...]
    eye = eye_ref[...]
    w0 = w0_ref[0]
    wdec = wdec_ref[0].astype(BF16)

    heads = [slice(RWKV_N * h, RWKV_N * (h + 1)) for h in range(RWKV_H)]
    each = lambda fn, *lists: [fn(*xs) for xs in zip(*lists)]
    nt = lambda a, b: _mm3s(a, b, _NT)
    tn = lambda a, b: _mm3s(a, b, _TN)

    rows_of, ar, bt, kt, vh, bh, kh, pc = [], [], [], [], [], [], [], []
    for j in range(NSUB):
        jj = jnp.where(d == 0, j, NSUB - 1 - j)
        rows = pl.ds(pl.multiple_of(jj * CHUNK, CHUNK), CHUNK)
        rows_of.append(rows)
        wr = w0 + _mm(tlw_ref[0, rows, :], wdec)
        lw = -jnp.exp(-_softplus(-wr) - 0.5)
        gc = _mm_mask_lhs(incl_b, lw)
        tot = jnp.sum(lw, axis=0, keepdims=True)
        eng = jnp.exp(-gc)
        e_end = jnp.exp(tot - gc)
        kkv = kk_ref[0, rows, :]
        abv = ab_ref[0, rows, :]
        k2 = k_ref[0, rows, :]
        vv = v_ref[0, rows, :]
        at_j = -kkv * jnp.exp(gc - lw)
        rt_j = r_ref[0, rows, :] * jnp.exp(gc)
        bt_j = abv * eng
        kt_j = k2 * eng
        bh_j = abv * e_end
        kh_j = k2 * e_end
        pc_j = jnp.exp(tot)
        for hs in heads:
            ar.append(_split2(jnp.concatenate([at_j[:, hs], rt_j[:, hs]], axis=0)))
            bt.append(_split2(bt_j[:, hs]))
            kt.append(_split2(kt_j[:, hs]))
            vh.append(_split2(vv[:, hs]))
            bh.append(_split2(bh_j[:, hs]))
            kh.append(_split2(kh_j[:, hs]))
            pc.append(pc_j[:, hs])

    g1 = each(nt, ar, bt)
    g2 = each(nt, ar, kt)
    a_ab = [g[:CHUNK] * strict for g in g1]
    a_rb = [_split2(g[CHUNK:] * incl) for g in g1]
    a_ak = [_split2(g[:CHUNK] * strict) for g in g2]
    a_rk = [_split2(g[CHUNK:] * incl) for g in g2]
    av = each(_mm3s, a_ak, vh)
    rkv = each(_mm3s, a_rk, vh)
    khv = each(tn, vh, kh)
    ad = [a * bd16 for a in a_ab]
    ee = [_split2(a - b) for a, b in zip(a_ab, ad)]
    p = [eye + a for a in ad]
    pw = [_split2(a) for a in ad]
    for _ in range(3):
        pw = [_split2(_mm3s(a, a)) for a in pw]
        p = [x + _mm3s(_split2(x), a) for x, a in zip(p, pw)]
    p_s = [_split2(x) for x in p]
    f = [_split2(_mm3s(x, e)) for x, e in zip(p_s, ee)]
    f2 = [_split2(_mm3s(x, x)) for x in f]
    t1 = [x + _mm3s(a, xs) for x, a, xs in zip(p, f, p_s)]
    tinv = [_split2(x + _mm3s(a, _split2(x))) for x, a in zip(t1, f2)]

    ht = [h_ref[h] for h in range(RWKV_H)]
    for j in range(NSUB):
        sl = slice(j * RWKV_H, (j + 1) * RWKV_H)
        ah = [nt(a, _split2(x)) for a, x in zip(ar[sl], ht)]
        uu = [_split2(_mm3s(t, _split2(a[:CHUNK] + b))) for t, a, b in zip(tinv[sl], ah, av[sl])]
        ys = [a[CHUNK:] + _mm3s(b, u) + c for a, b, u, c in zip(ah, a_rb[sl], uu, rkv[sl])]
        for hs, y in zip(heads, ys):
            y_ref[0, 0, rows_of[j], hs] = y
        ht = [x * c + tn(u, b) + e for x, c, u, b, e in zip(ht, pc[sl], uu, bh[sl], khv[sl])]
    for h in range(RWKV_H):
        h_ref[h] = ht[h]


def _rwkv_scan(r, k2, v, kk, ab, tlw, w0, wdec, tri, stri, bd16, eye, nbc):
    bsz, t, _ = r.shape
    nb = t // BLK
    blk = lambda b, d, i: _block_of_step(d, i, nbc, nb)
    tok = lambda w_: pl.BlockSpec((1, BLK, w_), lambda b, d, i: (b, blk(b, d, i), 0))
    full = lambda shape: pl.BlockSpec(shape, lambda b, d, i: (0,) * len(shape))
    return pl.pallas_call(
        _rwkv_kernel,
        grid=(bsz, 2, nb),
        in_specs=[tok(HEAD_W)] * 5 + [tok(LANE),
                  pl.BlockSpec((1, 1, HEAD_W), lambda b, d, i: (d, 0, 0)),
                  pl.BlockSpec((1, LANE, HEAD_W), lambda b, d, i: (d, 0, 0)),
                  full((2, CHUNK, CHUNK)), full((2, CHUNK, CHUNK)),
                  full((CHUNK, CHUNK)), full((CHUNK, CHUNK))],
        out_specs=pl.BlockSpec((1, 1, BLK, HEAD_W), lambda b, d, i: (d, b, blk(b, d, i), 0)),
        out_shape=jax.ShapeDtypeStruct((2, bsz, t, HEAD_W), F32),
        scratch_shapes=[pltpu.VMEM((RWKV_H, RWKV_N, RWKV_N), F32)],
        compiler_params=_cparams(("arbitrary", "arbitrary", "arbitrary")),
        name="rwkv_scan",
    )(r, k2, v, kk, ab, tlw, w0, wdec, tri, stri, bd16, eye)


def _rope(x, c, sa, sb):
    w = x.shape[1]
    return x * c + pltpu.roll(x, w - ROPE_AXIS_DIM // 2, 1) * sa + pltpu.roll(x, ROPE_AXIS_DIM // 2, 1) * sb


def _att_prep_kernel(u_ref, c_ref, sa_ref, sb_ref, qg_ref, kg_ref, bd_ref, q_ref, kt_ref, v_ref):
    u = u_ref[0]
    bd = bd_ref[...].astype(BF16)
    c, sa, sb = c_ref[...], sa_ref[...], sb_ref[...]
    c2 = jnp.concatenate([c, c], axis=1)
    sa2 = jnp.concatenate([sa, sa], axis=1)
    sb2 = jnp.concatenate([sb, sb], axis=1)
    q = u[:, 0:256]
    k = u[:, 256:384]
    qms = _mm_mask_rhs(q * q, bd) * (1.0 / ATT_HD)
    kms = _mm_mask_rhs(k * k, bd[:LANE, :LANE]) * (1.0 / ATT_HD)
    qn = q * lax.rsqrt(qms + EPS) * qg_ref[...]
    kn = k * lax.rsqrt(kms + EPS) * kg_ref[...]
    q_ref[0] = _rope(qn, c2, sa2, sb2) * (ATT_HD ** -0.5)
    knt = _rope(kn, c, sa, sb).T
    kt_ref[0, 0] = knt[:ATT_HD].astype(BF16)
    kt_ref[0, 1] = knt[ATT_HD:].astype(BF16)
    v_ref[0, 0] = u[:, 384:448].astype(BF16)
    v_ref[0, 1] = u[:, 448:512].astype(BF16)


def _att_prep(u, c, sa, sb, qg, kg, bd64):
    bsz, t, _ = u.shape
    nb = t // BLK
    full = lambda shape: pl.BlockSpec(shape, lambda b, i: (0,) * len(shape))
    tab = pl.BlockSpec((BLK, LANE), lambda b, i: (i, 0))
    return pl.pallas_call(
        _att_prep_kernel,
        grid=(bsz, nb),
        in_specs=[pl.BlockSpec((1, BLK, ATT_W), lambda b, i: (b, i, 0)), tab, tab, tab,
                  full((1, HEAD_W)), full((1, LANE)), full((HEAD_W, HEAD_W))],
        out_specs=[pl.BlockSpec((1, BLK, HEAD_W), lambda b, i: (b, i, 0)),
                   pl.BlockSpec((1, ATT_HKV, ATT_HD, BLK), lambda b, i: (b, 0, 0, i)),
                   pl.BlockSpec((1, ATT_HKV, BLK, ATT_HD), lambda b, i: (b, 0, i, 0))],
        out_shape=[jax.ShapeDtypeStruct((bsz, t, HEAD_W), F32),
                   jax.ShapeDtypeStruct((bsz, ATT_HKV, ATT_HD, t), BF16),
                   jax.ShapeDtypeStruct((bsz, ATT_HKV, t, ATT_HD), BF16)],
        compiler_params=_cparams(("arbitrary", "arbitrary")),
        name="att_prep",
    )(u, c, sa, sb, qg, kg, bd64)


def _att_kernel(q_ref, kt_ref, v_ref, o_ref, *, nbc, n_ctx):
    i = pl.program_id(2)
    q = q_ref[0]
    qs = jnp.concatenate([q[:, :ATT_HD], q[:, ATT_HD:]], axis=0).astype(BF16)
    s = jnp.dot(qs, kt_ref[0, 0], preferred_element_type=F32)
    t = s.shape[1]
    col = lax.broadcasted_iota(jnp.int32, s.shape, 1)
    limit = jnp.where(i < nbc, n_ctx, t)
    s = jnp.where(col < limit, s, -1e30)
    e = jnp.exp(s - jnp.max(s, axis=-1, keepdims=True))
    l = jnp.sum(e, axis=-1, keepdims=True)
    o = jnp.dot(e.astype(BF16), v_ref[0, 0], preferred_element_type=F32) / l
    o_ref[0, :, 0:ATT_HD] = o[:BLK]
    o_ref[0, :, ATT_HD:2 * ATT_HD] = o[BLK:]


def _attention(q, kt, v, nbc, n_ctx):
    bsz, t, _ = q.shape
    nb = t // BLK
    return pl.pallas_call(
        functools.partial(_att_kernel, nbc=nbc, n_ctx=n_ctx),
        grid=(bsz, ATT_HKV, nb),
        in_specs=[pl.BlockSpec((1, BLK, LANE), lambda b, g, i: (b, i, g)),
                  pl.BlockSpec((1, 1, ATT_HD, t), lambda b, g, i: (b, g, 0, 0)),
                  pl.BlockSpec((1, 1, t, ATT_HD), lambda b, g, i: (b, g, 0, 0))],
        out_specs=pl.BlockSpec((1, BLK, LANE), lambda b, g, i: (b, i, g)),
        out_shape=jax.ShapeDtypeStruct((bsz, t, HEAD_W), F32),
        compiler_params=_cparams(("arbitrary", "arbitrary", "arbitrary")),
        name="attention",
    )(q, kt, v)


def _out_kernel(h_ref, mod_ref, glaf_ref, glab_ref, glag_ref, ssdf_ref, ssdb_ref, xs_ref, z_ref,
                rwf_ref, rwb_ref, r_ref, k_ref, v_ref, g_ref, att_ref,
                glan_ref, ssd_d_ref, ssdn_ref, rk_ref, lng_ref, lnb_ref, bd_ref, w_ref, o_ref):
    bd = bd_ref[...].astype(BF16)
    seg_mean = lambda x: _mm_mask_rhs(x, bd) * (1.0 / 64.0)
    o = glaf_ref[0, 0] + glab_ref[0, 0]
    y_gla = o * lax.rsqrt(seg_mean(o * o) + EPS) * glan_ref[...] * _silu(glag_ref[0])
    y = ssdf_ref[0, 0] + ssdb_ref[0, 0] + ssd_d_ref[...] * xs_ref[0, 0]
    y = y * _silu(z_ref[0])
    y_ssd = y * lax.rsqrt(jnp.mean(y * y, axis=-1, keepdims=True) + EPS) * ssdn_ref[...]
    y = rwf_ref[0, 0] + rwb_ref[0, 0]
    mu = seg_mean(y)
    yc = y - mu
    var = seg_mean(yc * yc)
    yn = yc * lax.rsqrt(var + RWKV_GN_EPS) * lng_ref[...] + lnb_ref[...]
    v = v_ref[0]
    bonus = _mm_mask_rhs(r_ref[0] * k_ref[0] * rk_ref[...], bd) * v
    y_rw = (yn + bonus) * g_ref[0]
    w = w_ref[...]
    proj = (_mm(y_gla, w[0:256]) + _mm(y_ssd, w[256:512])) + (_mm(y_rw, w[512:768]) + _mm(att_ref[0], w[768:1024]))
    o_ref[0] = h_ref[0] + mod_ref[0][5:6] * proj


def _mix_out(h, mod, gla_o, u_gla, ssd_y, ssd_xs, u_ssm, rw_y, r, k2, v, g, att_o,
             gla_n, ssd_d, ssd_n, r_k, ln_g, ln_b, bd64, w_out, nbc):
    bsz, t, d = h.shape
    tok = lambda w_, col=0: pl.BlockSpec((1, BLK, w_), lambda b, i: (b, i, col))
    dirn = lambda dd: pl.BlockSpec((1, 1, BLK, HEAD_W), lambda b, i: (dd, b, i, 0))
    full = lambda shape: pl.BlockSpec(shape, lambda b, i: (0,) * len(shape))
    vec = full((1, HEAD_W))
    return pl.pallas_call(
        _out_kernel,
        grid=(bsz, t // BLK),
        in_specs=[tok(d), pl.BlockSpec((1, N_MOD, d), lambda b, i: (_mod_row(b, i, nbc), 0, 0)),
                  dirn(0), dirn(1), tok(HEAD_W, 2),
                  dirn(0), dirn(1), dirn(0), tok(HEAD_W, 0),
                  dirn(0), dirn(1), tok(HEAD_W), tok(HEAD_W), tok(HEAD_W), tok(HEAD_W), tok(HEAD_W),
                  vec, vec, vec, vec, vec, vec, full((HEAD_W, HEAD_W)), full((d, d))],
        out_specs=tok(d),
        out_shape=jax.ShapeDtypeStruct(h.shape, F32),
        compiler_params=_cparams(("arbitrary", "arbitrary")),
        name="mix_out",
    )(h, mod, gla_o, gla_o, u_gla, ssd_y, ssd_y, ssd_xs, u_ssm, rw_y, rw_y, r, k2, v, g, att_o,
      gla_n, ssd_d, ssd_n, r_k, ln_g, ln_b, bd64, w_out)


def _rope_tables(n_ctx, n_lat):
    rows = n_lat // GRID_W
    row = jnp.repeat(jnp.arange(rows, dtype=F32), GRID_W)
    col = jnp.tile(jnp.arange(GRID_W, dtype=F32), rows)
    inv = ROPE_THETA ** (-jnp.arange(0, ROPE_AXIS_DIM, 2, dtype=F32) / ROPE_AXIS_DIM)
    ang = jnp.stack([row[:, None] * inv, col[:, None] * inv], axis=1)
    cos, sin = jnp.cos(ang), jnp.sin(ang)
    zero = jnp.zeros_like(sin)
    c = jnp.concatenate([cos, cos], axis=-1).reshape(n_lat, ATT_HD)
    sa = jnp.concatenate([-sin, zero], axis=-1).reshape(n_lat, ATT_HD)
    sb = jnp.concatenate([zero, sin], axis=-1).reshape(n_lat, ATT_HD)
    pad = lambda x, fill: jnp.concatenate([jnp.full((n_ctx, ATT_HD), fill, F32), x], axis=0)
    two = lambda x: jnp.concatenate([x, x], axis=1)
    return two(pad(c, 1.0)), two(pad(sa, 0.0)), two(pad(sb, 0.0))


def _masks():
    t = jnp.arange(CHUNK)
    lower = (t[None, :] <= t[:, None]).astype(F32)
    slower = (t[None, :] < t[:, None]).astype(F32)
    tri = jnp.stack([lower, lower.T])
    stri = jnp.stack([slower, slower.T])
    bd16 = (t[None, :] // SUB == t[:, None] // SUB).astype(F32)
    eye = jnp.eye(CHUNK, dtype=F32)
    c = jnp.arange(HEAD_W)
    bd64 = (c[None, :] // 64 == c[:, None] // 64).astype(F32)
    expand = (jnp.arange(LANE)[:, None] == c[None, :] // 64).astype(F32)
    return tri, stri, bd16, eye, bd64, expand


def _pad_cols(x, width):
    return jnp.pad(x, [(0, 0)] * (x.ndim - 1) + [(0, width - x.shape[-1])])


def _pad_rows(x, height, top=0):
    return jnp.pad(x, [(0, 0)] * (x.ndim - 2) + [(top, height - top - x.shape[-2]), (0, 0)])


def _rep(x, n):
    return jnp.repeat(x, n, axis=-1)


def kernel(x, c, ctx, c_ctx, norm_g, w_mod, b_mod, ffn_in, ffn_out, w_in, w_out, gla_w_dec, gla_b_dec, gla_norm, ssm_conv_w, ssm_conv_b, ssm_dt_bias, ssm_a_log, ssm_d, ssm_norm, rwkv_shift_mu, rwkv_w0, rwkv_w_dec, rwkv_a0, rwkv_w_a, rwkv_w_g, rwkv_k_k, rwkv_k_a, rwkv_r_k, rwkv_ln_g, rwkv_ln_b, att_q_norm, att_k_norm):
    bsz, n_lat, d = x.shape
    n_ctx = ctx.shape[1]
    depth = w_mod.shape[0]
    assert bsz == 4 and n_ctx % BLK == 0 and n_lat % BLK == 0 and n_lat % GRID_W == 0
    nbc = n_ctx // BLK
    tri, stri, bd16, eye, bd64, expand = _masks()
    rope_c, rope_sa, rope_sb = _rope_tables(n_ctx, n_lat)

    cvec = jnp.concatenate([c, c_ctx[None], jnp.zeros((8 - bsz - 1, d), F32)], axis=0)
    mod_all = _compute_mod(cvec, w_mod, b_mod).reshape(depth, 8, N_MOD, d)

    h = jnp.concatenate([ctx, x], axis=1)
    for l in range(depth):
        mod = mod_all[l]
        w1 = ffn_in[l].astype(BF16)
        w2 = ffn_out[l].astype(BF16)
        wi = w_in[l]
        o_ssm = 784
        o_rw = o_ssm + 772
        o_att = o_rw + 1024
        w_proj = jnp.concatenate([_pad_cols(wi[:, :o_ssm], GLA_W), _pad_cols(wi[:, o_ssm:o_rw], SSM_W),
                                  wi[:, o_rw:o_att], wi[:, o_att:]], axis=1).astype(BF16)
        gla_wd = _pad_rows(gla_w_dec[l], LANE)
        gla_bd = gla_b_dec[l][:, None, :]
        ssm_dtb = _rep(ssm_dt_bias[l], SSM_P)[:, None, :]
        ssm_an = _rep(-jnp.exp(ssm_a_log[l]), SSM_P)[:, None, :]
        rw_wd = _pad_rows(rwkv_w_dec[l], LANE)
        rw_wa = _pad_rows(rwkv_w_a[l], LANE, top=64)
        rw_w0 = rwkv_w0[l][:, None, :]

        h = _ffn(h, mod, norm_g[l, 0][None], w1[0], w2[0], 0, nbc)
        u_gla, u_ssm, u_rw, u_att = _inproj(h, mod, norm_g[l, 1][None], w_proj, nbc)

        gla_o = _gla_scan(u_gla, gla_wd, gla_bd, tri, nbc)
        ssd_y, ssd_xs = _ssd_scan(u_ssm, ssm_conv_w[l], ssm_conv_b[l][None], ssm_dtb, ssm_an, expand, tri, nbc)
        r, k2, v, g, kk, ab, tlw = _rwkv_prep(u_rw, rwkv_shift_mu[l], rwkv_a0[l][None], rw_wa, rwkv_w_g[l],
                                              rwkv_k_k[l][None], rwkv_k_a[l][None], bd64, nbc)
        rw_y = _rwkv_scan(r, k2, v, kk, ab, tlw, rw_w0, rw_wd, tri, stri, bd16, eye, nbc)
        qn, kt, vt = _att_prep(u_att, rope_c, rope_sa, rope_sb, jnp.tile(att_q_norm[l], ATT_HQ)[None],
                               jnp.tile(att_k_norm[l], ATT_HKV)[None], bd64)
        att_o = _attention(qn, kt, vt, nbc, n_ctx)

        h = _mix_out(h, mod, gla_o, u_gla, ssd_y, ssd_xs, u_ssm, rw_y, r, k2, v, g, att_o,
                     jnp.tile(gla_norm[l], GLA_H)[None], _rep(ssm_d[l], SSM_P)[None], ssm_norm[l][None],
                     rwkv_r_k[l].reshape(1, HEAD_W), rwkv_ln_g[l][None], rwkv_ln_b[l][None], bd64,
                     w_out[l].astype(BF16), nbc)
        h = _ffn(h, mod, norm_g[l, 2][None], w1[1], w2[1], 6, nbc)
    return h[:, n_ctx:]
```

```python
import functools
import math

import jax
import jax.numpy as jnp
from jax import lax
from jax.experimental import pallas as pl
from jax.experimental.pallas import tpu as pltpu

F32 = jnp.float32
BF16 = jnp.bfloat16

N_MOD = 9
EPS = 1e-6
GLA_H, GLA_DK, GLA_DV, GLA_LORA, GLA_TAU = 4, 32, 64, 16, 16.0
SSM_H, SSM_P, SSM_G, SSM_N = 4, 64, 2, 64
RWKV_H, RWKV_N, RWKV_GN_EPS = 4, 64, 64e-5
ATT_HQ, ATT_HKV, ATT_HD = 4, 2, 64
GRID_W = 64
ROPE_THETA = 10000.0
ROPE_AXIS_DIM = ATT_HD // 2

CHUNK = 64
SUB = 16
BLK = 256
NSUB = BLK // CHUNK
ATT_ROWS = 128
LANE = 128
HEAD_W = 256
VMEM_LIMIT = 56 * 1024 * 1024

GLA_W = 896
SSM_W = 896
RWKV_W = 1024
ATT_W = 512


def _cparams(sem):
    return pltpu.CompilerParams(dimension_semantics=sem, vmem_limit_bytes=VMEM_LIMIT)


def _mm(a, b):
    return jnp.dot(a.astype(BF16), b.astype(BF16), preferred_element_type=F32)


def _dg(a, b, dims):
    return lax.dot_general(a, b, (dims, ((), ())), preferred_element_type=F32)


_NN = ((1,), (0,))
_NT = ((1,), (1,))
_TN = ((0,), (0,))


def _split2(x):
    hi = x.astype(BF16)
    lo = (x - hi.astype(F32)).astype(BF16)
    return hi, lo


def _split3(x):
    hi = x.astype(BF16)
    r1 = x - hi.astype(F32)
    mid = r1.astype(BF16)
    lo = (r1 - mid.astype(F32)).astype(BF16)
    return hi, mid, lo


def _mm3(a, b, dims=_NN):
    ah, al = _split2(a)
    bh, bl = _split2(b)
    return _dg(ah, bh, dims) + (_dg(ah, bl, dims) + _dg(al, bh, dims))


def _mm3s(a, b, dims=_NN):
    return _dg(a[0], b[0], dims) + (_dg(a[0], b[1], dims) + _dg(a[1], b[0], dims))


_parts = _split2


def _kdot(a, b):
    ah, al = a
    bh, bl = b
    return _dg(jnp.concatenate([ah, al], axis=1), jnp.concatenate([bh, bl, bh, bl], axis=0), _NN)


def _mm_mask_lhs(mask_bf16, x, dims=_NN):
    hi, mid, lo = _split3(x)
    return _dg(mask_bf16, hi, dims) + (_dg(mask_bf16, mid, dims) + _dg(mask_bf16, lo, dims))


def _mm_mask_rhs(x, mask_bf16, dims=_NN):
    hi, mid, lo = _split3(x)
    return _dg(hi, mask_bf16, dims) + (_dg(mid, mask_bf16, dims) + _dg(lo, mask_bf16, dims))


def _sigmoid(x):
    return jax.nn.sigmoid(x)


def _silu(x):
    return x * jax.nn.sigmoid(x)


def _softplus(x):
    return jnp.maximum(x, 0.0) + jnp.log1p(jnp.exp(-jnp.abs(x)))


def _modulate(x, g, shift, scale):
    ms = jnp.mean(x * x, axis=-1, keepdims=True)
    return (x * lax.rsqrt(ms + EPS) * g) * (1.0 + scale) + shift


def _block_of_step(d, i, nbc, nb):
    back = jnp.where(i < nbc, nbc - 1 - i, nb + nbc - 1 - i)
    return jnp.where(d == 0, i, back)


def _mod_row(b, i, nbc):
    return jnp.where(i < nbc, 4, b)


def _mod_kernel(c_ref, w_ref, b_ref, o_ref):
    o_ref[0] = _mm(_silu(c_ref[...]), w_ref[0]) + b_ref[0]


def _compute_mod(cvec, w_mod, b_mod):
    depth, d, n = w_mod.shape
    tn = 1024
    return pl.pallas_call(
        _mod_kernel,
        grid=(depth, n // tn),
        in_specs=[pl.BlockSpec((8, d), lambda l, j: (0, 0)),
                  pl.BlockSpec((1, d, tn), lambda l, j: (l, 0, j)),
                  pl.BlockSpec((1, 1, tn), lambda l, j: (l, 0, j))],
        out_specs=pl.BlockSpec((1, 8, tn), lambda l, j: (l, 0, j)),
        out_shape=jax.ShapeDtypeStruct((depth, 8, n), F32),
        compiler_params=_cparams(("arbitrary", "arbitrary")),
        name="adaln_mod",
    )(cvec, w_mod, b_mod.reshape(depth, 1, n))


def _ffn_kernel(h_ref, mod_ref, g_ref, w1_ref, w2_ref, o_ref, *, i0, d_ff):
    x = h_ref[0]
    m = mod_ref[0]
    hm = _modulate(x, g_ref[...], m[i0:i0 + 1], m[i0 + 1:i0 + 2]).astype(BF16)
    ab = jnp.dot(hm, w1_ref[...], preferred_element_type=F32)
    a = ab[:, :d_ff]
    act = (_silu(a) * ab[:, d_ff:]).astype(BF16)
    y = jnp.dot(act, w2_ref[...], preferred_element_type=F32)
    o_ref[0] = x + (0.5 * m[i0 + 2:i0 + 3]) * y


def _ffn(h, mod, g, w1, w2, i0, nbc):
    bsz, t, d = h.shape
    d_ff = w2.shape[0]
    return pl.pallas_call(
        functools.partial(_ffn_kernel, i0=i0, d_ff=d_ff),
        grid=(bsz, t // BLK),
        in_specs=[pl.BlockSpec((1, BLK, d), lambda b, i: (b, i, 0)),
                  pl.BlockSpec((1, N_MOD, d), lambda b, i: (_mod_row(b, i, nbc), 0, 0)),
                  pl.BlockSpec((1, d), lambda b, i: (0, 0)),
                  pl.BlockSpec((d, 2 * d_ff), lambda b, i: (0, 0)),
                  pl.BlockSpec((d_ff, d), lambda b, i: (0, 0))],
        out_specs=pl.BlockSpec((1, BLK, d), lambda b, i: (b, i, 0)),
        out_shape=jax.ShapeDtypeStruct(h.shape, F32),
        compiler_params=_cparams(("arbitrary", "arbitrary")),
        name="ffn",
    )(h, mod, g, w1, w2)


def _inproj_kernel(h_ref, mod_ref, g_ref, w_ref, gla_ref, ssm_ref, rwkv_ref, att_ref):
    m = mod_ref[0]
    hm = _modulate(h_ref[0], g_ref[...], m[3:4], m[4:5]).astype(BF16)
    u = jnp.dot(hm, w_ref[...], preferred_element_type=F32)
    gla_ref[0] = u[:, :GLA_W]
    ssm_ref[0] = u[:, GLA_W:GLA_W + SSM_W]
    rwkv_ref[0] = u[:, GLA_W + SSM_W:GLA_W + SSM_W + RWKV_W]
    att_ref[0] = u[:, GLA_W + SSM_W + RWKV_W:]


def _inproj(h, mod, g, w, nbc):
    bsz, t, d = h.shape
    widths = (GLA_W, SSM_W, RWKV_W, ATT_W)
    return pl.pallas_call(
        _inproj_kernel,
        grid=(bsz, t // BLK),
        in_specs=[pl.BlockSpec((1, BLK, d), lambda b, i: (b, i, 0)),
                  pl.BlockSpec((1, N_MOD, d), lambda b, i: (_mod_row(b, i, nbc), 0, 0)),
                  pl.BlockSpec((1, d), lambda b, i: (0, 0)),
                  pl.BlockSpec((d, sum(widths)), lambda b, i: (0, 0))],
        out_specs=[pl.BlockSpec((1, BLK, w_), lambda b, i: (b, i, 0)) for w_ in widths],
        out_shape=[jax.ShapeDtypeStruct((bsz, t, w_), F32) for w_ in widths],
        compiler_params=_cparams(("arbitrary", "arbitrary")),
        name="inproj",
    )(h, mod, g, w)


def _gla_kernel(u_ref, wdec_ref, bdec_ref, tri_ref, o_ref, s_ref):
    d = pl.program_id(1)
    i = pl.program_id(2)

    @pl.when(i == 0)
    def _():
        s_ref[...] = jnp.zeros_like(s_ref)

    tri = tri_ref[d]
    tri_b = tri.astype(BF16)
    wdec = wdec_ref[0].astype(BF16)
    bdec = bdec_ref[0]

    rows_of, qh, kih, keh, vh, dec = [], [], [], [], [], []
    for j in range(NSUB):
        jj = jnp.where(d == 0, j, NSUB - 1 - j)
        rows = pl.ds(pl.multiple_of(jj * CHUNK, CHUNK), CHUNK)
        rows_of.append(rows)
        u = u_ref[0, rows, :]
        q = u[:, 0:128] * (GLA_DK ** -0.5)
        k = u[:, 128:256]
        v = u[:, 256:512]
        z = _mm(u[:, 768:896], wdec) + bdec
        la = -_softplus(-z) * (1.0 / GLA_TAU)
        bc = _mm_mask_lhs(tri_b, la)
        bt = jnp.sum(la, axis=0, keepdims=True)
        qd = q * jnp.exp(bc)
        ki = k * jnp.exp(-bc)
        ke = k * jnp.exp(bt - bc)
        dec_j = jnp.exp(bt)
        for h in range(GLA_H):
            ks = slice(GLA_DK * h, GLA_DK * (h + 1))
            qh.append(qd[:, ks].astype(BF16))
            kih.append(ki[:, ks].astype(BF16))
            keh.append(ke[:, ks].astype(BF16))
            vh.append(v[:, GLA_DV * h:GLA_DV * (h + 1)].astype(BF16))
            dec.append(dec_j[:, ks])
    att = [_dg(a, b, _NT) * tri for a, b in zip(qh, kih)]
    o_intra = [_mm(a, b) for a, b in zip(att, vh)]
    kv = [_dg(a, b, _TN) for a, b in zip(vh, keh)]
    st = [s_ref[h] for h in range(GLA_H)]
    s_prev = []
    for j in range(NSUB):
        for h in range(GLA_H):
            n = j * GLA_H + h
            s_prev.append(st[h].astype(BF16))
            st[h] = st[h] * dec[n] + kv[n]
    for h in range(GLA_H):
        s_ref[h] = st[h]
    for j in range(NSUB):
        for h in range(GLA_H):
            n = j * GLA_H + h
            o_ref[0, 0, rows_of[j], GLA_DV * h:GLA_DV * (h + 1)] = o_intra[n] + _dg(qh[n], s_prev[n], _NT)


def _gla_scan(u, wdec, bdec, tri, nbc):
    bsz, t, _ = u.shape
    nb = t // BLK
    blk = lambda b, d, i: _block_of_step(d, i, nbc, nb)
    return pl.pallas_call(
        _gla_kernel,
        grid=(bsz, 2, nb),
        in_specs=[pl.BlockSpec((1, BLK, GLA_W), lambda b, d, i: (b, blk(b, d, i), 0)),
                  pl.BlockSpec((1, LANE, LANE), lambda b, d, i: (d, 0, 0)),
                  pl.BlockSpec((1, 1, LANE), lambda b, d, i: (d, 0, 0)),
                  pl.BlockSpec((2, CHUNK, CHUNK), lambda b, d, i: (0, 0, 0))],
        out_specs=pl.BlockSpec((1, 1, BLK, HEAD_W), lambda b, d, i: (d, b, blk(b, d, i), 0)),
        out_shape=jax.ShapeDtypeStruct((2, bsz, t, HEAD_W), F32),
        scratch_shapes=[pltpu.VMEM((GLA_H, GLA_DV, GLA_DK), F32)],
        compiler_params=_cparams(("arbitrary", "arbitrary", "arbitrary")),
        name="gla_scan",
    )(u, wdec, bdec, tri)


def _shifted(x, prev_row, next_row):
    n = x.shape[0]
    row = lax.broadcasted_iota(jnp.int32, x.shape, 0)
    xm1 = jnp.where(row == 0, prev_row, pltpu.roll(x, 1, 0))
    xp1 = jnp.where(row == n - 1, next_row, pltpu.roll(x, n - 1, 0))
    return xm1, xp1


def _halo_specs(width, blk_fn):
    per = BLK // 8

    def prev_map(*idx):
        b, blk = blk_fn(*idx)
        return (b, jnp.maximum(blk * per - 1, 0), 0)

    def next_map(nrow8):
        def f(*idx):
            b, blk = blk_fn(*idx)
            return (b, jnp.minimum((blk + 1) * per, nrow8 - 1), 0)
        return f

    return (lambda: pl.BlockSpec((1, 8, width), prev_map),
            lambda nrow8: pl.BlockSpec((1, 8, width), next_map(nrow8)))


def _stream_edges(blk, nbc, nb):
    has_prev = jnp.logical_and(blk != 0, blk != nbc)
    has_next = jnp.logical_and(blk != nbc - 1, blk != nb - 1)
    return has_prev.astype(F32), has_next.astype(F32)


def _ssd_kernel(u_ref, up_ref, un_ref, cw_ref, cb_ref, dtb_ref, aneg_ref, exp_ref, tri_ref,
                y_ref, xs_ref, h_ref, xdt_s, bm_s, cm_s, da_s, *, nbc, nb):
    d = pl.program_id(1)
    i = pl.program_id(2)
    blk = _block_of_step(d, i, nbc, nb)

    @pl.when(i == 0)
    def _():
        h_ref[...] = jnp.zeros_like(h_ref)

    has_prev, has_next = _stream_edges(blk, nbc, nb)
    u = u_ref[0]
    xbc = u[:, 256:768]
    prev_row = up_ref[0, 7:8, 256:768] * has_prev
    next_row = un_ref[0, 0:1, 256:768] * has_next
    xm1, xp1 = _shifted(xbc, prev_row, next_row)
    cw = cw_ref[...]
    act = _silu(cw[0:1] * xm1 + cw[1:2] * xbc + cw[2:3] * xp1 + cb_ref[...])
    xs = act[:, 0:256]
    xs_ref[0, 0] = xs
    dt = _softplus(_mm_mask_rhs(u[:, 768:896], exp_ref[...].astype(BF16)) + dtb_ref[0])
    xdt_s[...] = xs * dt
    da_s[...] = dt * aneg_ref[0]
    bm_s[...] = act[:, 256:384]
    cm_s[...] = act[:, 384:512]

    tri = tri_ref[d]
    tri_b = tri.astype(BF16)
    tri_o = tri_ref[1 - d].astype(BF16)

    per_g = SSM_H // SSM_G
    rows_of, cgs, bgs, das, xh, cum_col, dec = [], [], [], [], [], [], []
    for j in range(NSUB):
        jj = jnp.where(d == 0, j, NSUB - 1 - j)
        rows = pl.ds(pl.multiple_of(jj * CHUNK, CHUNK), CHUNK)
        rows_of.append(rows)
        da = da_s[rows, :]
        xdt = xdt_s[rows, :]
        bm = bm_s[rows, :]
        cm = cm_s[rows, :]
        cum = _mm_mask_lhs(tri_b, da)
        tot = jnp.sum(da, axis=0, keepdims=True)
        for h in range(SSM_H):
            hs = slice(SSM_P * h, SSM_P * (h + 1))
            gs = slice(SSM_N * (h // per_g), SSM_N * (h // per_g + 1))
            cgs.append(cm[:, gs])
            bgs.append(bm[:, gs])
            das.append(da[:, hs])
            xh.append(xdt[:, hs].astype(BF16))
            cum_col.append(cum[:, hs])
            dec.append(tot[:, hs])
    cb = [_dg(c.astype(BF16), b.astype(BF16), _NT) for c, b in zip(cgs[::per_g], bgs[::per_g])]
    cum_row = [_mm_mask_rhs(a, tri_o, _TN) for a in das]
    seg = [jnp.where(tri > 0, jnp.exp(c - r), 0.0) for c, r in zip(cum_col, cum_row)]
    y_intra = [_mm(cb[n // per_g] * s, x) for n, (s, x) in enumerate(zip(seg, xh))]
    contrib = [_dg((b * jnp.exp(t - c)).astype(BF16), x, _TN) for b, t, c, x in zip(bgs, dec, cum_col, xh)]
    hp = [h_ref[h] for h in range(SSM_H)]
    h_prev = []
    for j in range(NSUB):
        for h in range(SSM_H):
            n = j * SSM_H + h
            h_prev.append(hp[h])
            hp[h] = hp[h] * jnp.exp(dec[n]) + contrib[n]
    for h in range(SSM_H):
        h_ref[h] = hp[h]
    for j in range(NSUB):
        for h in range(SSM_H):
            n = j * SSM_H + h
            y_ref[0, 0, rows_of[j], SSM_P * h:SSM_P * (h + 1)] = (
                y_intra[n] + _mm(cgs[n] * jnp.exp(cum_col[n]), h_prev[n]))


def _ssd_scan(u, conv_w, conv_b, dtb, aneg, expand, tri, nbc):
    bsz, t, _ = u.shape
    nb = t // BLK
    blk = lambda b, d, i: _block_of_step(d, i, nbc, nb)
    prev_spec, next_spec = _halo_specs(SSM_W, lambda b, d, i: (b, blk(b, d, i)))
    full = lambda shape: pl.BlockSpec(shape, lambda b, d, i: (0,) * len(shape))
    return pl.pallas_call(
        functools.partial(_ssd_kernel, nbc=nbc, nb=nb),
        grid=(bsz, 2, nb),
        in_specs=[pl.BlockSpec((1, BLK, SSM_W), lambda b, d, i: (b, blk(b, d, i), 0)),
                  prev_spec(), next_spec(t // 8),
                  full((3, 512)), full((1, 512)),
                  pl.BlockSpec((1, 1, HEAD_W), lambda b, d, i: (d, 0, 0)),
                  pl.BlockSpec((1, 1, HEAD_W), lambda b, d, i: (d, 0, 0)),
                  full((LANE, HEAD_W)), full((2, CHUNK, CHUNK))],
        out_specs=[pl.BlockSpec((1, 1, BLK, HEAD_W), lambda b, d, i: (d, b, blk(b, d, i), 0))] * 2,
        out_shape=[jax.ShapeDtypeStruct((2, bsz, t, HEAD_W), F32)] * 2,
        scratch_shapes=[pltpu.VMEM((SSM_H, SSM_N, SSM_P), F32),
                        pltpu.VMEM((BLK, HEAD_W), F32), pltpu.VMEM((BLK, LANE), F32),
                        pltpu.VMEM((BLK, LANE), F32), pltpu.VMEM((BLK, HEAD_W), F32)],
        compiler_params=_cparams(("arbitrary", "arbitrary", "arbitrary")),
        name="ssd_scan",
    )(u, u, u, conv_w, conv_b, dtb, aneg, expand, tri)


def _rwkv_prep_kernel(u_ref, up_ref, un_ref, mu_ref, a0_ref, wa_ref, wg_ref, kk_ref, ka_ref, bd_ref,
                      r_ref, k_ref, v_ref, g_ref, kkn_ref, ab_ref, tlw_ref, *, nbc, nb):
    blk = pl.program_id(1)
    has_prev, has_next = _stream_edges(blk, nbc, nb)
    u = u_ref[0]
    xm1, xp1 = _shifted(u, up_ref[0, 7:8, :] * has_prev, un_ref[0, 0:1, :] * has_next)
    mu = mu_ref[...]
    u = u + mu[0:1] * (xm1 - u) + mu[1:2] * (xp1 - u)
    r = u[:, 0:256]
    k = u[:, 256:512]
    lwa = u[:, 768:896]
    a = _sigmoid(a0_ref[...] + _mm(lwa, wa_ref[...]))
    g = _mm(_sigmoid(u[:, 896:1024]), wg_ref[...])
    kk = k * kk_ref[...]
    ss = _mm_mask_rhs(kk * kk, bd_ref[...].astype(BF16))
    kk = kk / jnp.maximum(jnp.sqrt(ss), 1e-12)
    r_ref[0] = r
    k_ref[0] = k * (1.0 + (a - 1.0) * ka_ref[...])
    v_ref[0] = u[:, 512:768]
    g_ref[0] = g
    kkn_ref[0] = kk
    ab_ref[0] = kk * a
    tlw_ref[0] = jnp.tanh(lwa)


def _rwkv_prep(u, mu, a0, wa, wg, k_k, k_a, bd64, nbc):
    bsz, t, _ = u.shape
    nb = t // BLK
    prev_spec, next_spec = _halo_specs(RWKV_W, lambda b, i: (b, i))
    full = lambda shape: pl.BlockSpec(shape, lambda b, i: (0,) * len(shape))
    tok = lambda w_: pl.BlockSpec((1, BLK, w_), lambda b, i: (b, i, 0))
    return pl.pallas_call(
        functools.partial(_rwkv_prep_kernel, nbc=nbc, nb=nb),
        grid=(bsz, nb),
        in_specs=[tok(RWKV_W), prev_spec(), next_spec(t // 8),
                  full((2, RWKV_W)), full((1, HEAD_W)), full((LANE, HEAD_W)), full((LANE, HEAD_W)),
                  full((1, HEAD_W)), full((1, HEAD_W)), full((HEAD_W, HEAD_W))],
        out_specs=[tok(HEAD_W)] * 6 + [tok(LANE)],
        out_shape=[jax.ShapeDtypeStruct((bsz, t, HEAD_W), F32)] * 6
        + [jax.ShapeDtypeStruct((bsz, t, LANE), F32)],
        compiler_params=_cparams(("arbitrary", "arbitrary")),
        name="rwkv_prep",
    )(u, u, u, mu, a0, wa, wg, k_k, k_a, bd64)


def _rwkv_kernel(r_ref, k_ref, v_ref, kk_ref, ab_ref, tlw_ref, w0_ref, wdec_ref, tri_ref, stri_ref,
                 bd16_ref, eye_ref, y_ref, h_ref):
    d = pl.program_id(1)
    i = pl.program_id(2)

    @pl.when(i == 0)
    def _():
        h_ref[...] = jnp.zeros_like(h_ref)

    incl2 = tri_ref[d]
    strict2 = stri_ref[d]
    incl = incl2[:, :CHUNK]
    strict = strict2[:, :CHUNK]
    incl_b = incl.astype(BF16)
    bd16 = bd16_ref[...]
    eye = eye_ref[...]
    w0 = w0_ref[0]
    wdec = wdec_ref[0].astype(BF16)

    mask2 = jnp.concatenate([strict, incl], axis=0)
    stack = lambda a, b: jnp.concatenate([a, b], axis=0)
    heads = [slice(RWKV_N * h, RWKV_N * (h + 1)) for h in range(RWKV_H)]
    bf = lambda x: x.astype(BF16)

    rows_of, ar, rt, bt, kt, vh, bh, kh, pc = [], [], [], [], [], [], [], [], []
    for j in range(NSUB):
        jj = jnp.where(d == 0, j, NSUB - 1 - j)
        rows = pl.ds(pl.multiple_of(jj * CHUNK, CHUNK), CHUNK)
        rows_of.append(rows)
        wr = w0 + _mm(tlw_ref[0, rows, :], wdec)
        lw = -jnp.exp(-_softplus(-wr) - 0.5)
        gc = _mm_mask_lhs(incl_b, lw)
        tot = jnp.sum(lw, axis=0, keepdims=True)
        eng = jnp.exp(-gc)
        e_end = jnp.exp(tot - gc)
        abv = ab_ref[0, rows, :]
        k2 = k_ref[0, rows, :]
        at_j = bf(-kk_ref[0, rows, :] * jnp.exp(gc - lw))
        rt_f = r_ref[0, rows, :] * jnp.exp(gc)
        rt_j = bf(rt_f)
        bt_j = bf(abv * eng)
        kt_j = bf(k2 * eng)
        v_j = bf(v_ref[0, rows, :])
        bh_j = bf(abv * e_end)
        kh_j = bf(k2 * e_end)
        pc_j = jnp.exp(tot)
        for hs in heads:
            ar.append(stack(at_j[:, hs], rt_j[:, hs]))
            rt.append(rt_f[:, hs])
            bt.append(stack(bt_j[:, hs], bt_j[:, hs]))
            kt.append(kt_j[:, hs])
            vh.append(v_j[:, hs])
            bh.append(bh_j[:, hs])
            kh.append(kh_j[:, hs])
            pc.append(pc_j[:, hs])

    g1 = [_dg(a, b, _NT) for a, b in zip(ar, bt)]
    g2 = [_dg(a, b, _NT) for a, b in zip(ar, kt)]
    a_ab = [g[:CHUNK] * strict2 for g in g1]
    a_rb = [bf(g[CHUNK:, :CHUNK] * incl) for g in g1]
    avr = [_dg(bf(g * mask2), v, _NN) for g, v in zip(g2, vh)]
    khv = [_dg(v, k, _TN) for v, k in zip(vh, kh)]
    ad = [a * bd16 for a in a_ab]
    ee = [_parts(a - b) for a, b in zip(a_ab, ad)]
    q = [eye + a for a in ad]
    pw = [_kdot(x, x) for x in map(_parts, ad)]
    for _ in range(2):
        res = [_kdot(_parts(stack(a, x)), _parts(a)) for a, x in zip(pw, q)]
        pw = [r[:CHUNK] for r in res]
        q = [x + r[CHUNK:] for x, r in zip(q, res)]
    p = [x + _kdot(_parts(x), _parts(a)) for x, a in zip(q, pw)]
    p_s = [_parts(x) for x in p]
    f = [_parts(_kdot(x, e)) for x, e in zip(p_s, ee)]
    f2 = [_parts(_kdot(x, x)) for x in f]
    t1 = [x + _kdot(a, xs) for x, a, xs in zip(p, f, p_s)]
    tinv = [bf((x + _kdot(a, _parts(x)))[:, :CHUNK]) for x, a in zip(t1, f2)]
    z = [jnp.concatenate([a[:CHUNK], bf(c[:CHUNK])], axis=1) for a, c in zip(ar, avr)]
    tz = [bf(_dg(t, x, _NN)) for t, x in zip(tinv, z)]
    rz = [_dg(b, x, _NN) for b, x in zip(a_rb, tz)]
    w = [bf(_dg(t, b, _TN)) for t, b in zip(tinv, bh)]
    mn = [_dg(x, y, _TN) for x, y in zip(z, w)]
    qm = [bf(r + x[:, :CHUNK]) for r, x in zip(rt, rz)]
    y0 = [c[CHUNK:] + x[:, CHUNK:] for c, x in zip(avr, rz)]
    m1 = [bf(x[:CHUNK]) for x in mn]
    n1 = [x[CHUNK:] + e for x, e in zip(mn, khv)]

    ht = [h_ref[h] for h in range(RWKV_H)]
    for j in range(NSUB):
        for h in range(RWKV_H):
            n = j * RWKV_H + h
            hb = bf(ht[h])
            y_ref[0, 0, rows_of[j], heads[h]] = _dg(qm[n], hb, _NT) + y0[n]
            ht[h] = ht[h] * pc[n] + _dg(hb, m1[n], _NN) + n1[n]
    for h in range(RWKV_H):
        h_ref[h] = ht[h]


def _rwkv_scan(r, k2, v, kk, ab, tlw, w0, wdec, tri, stri, bd16, eye, nbc):
    bsz, t, _ = r.shape
    nb = t // BLK
    blk = lambda b, d, i: _block_of_step(d, i, nbc, nb)
    tok = lambda w_: pl.BlockSpec((1, BLK, w_), lambda b, d, i: (b, blk(b, d, i), 0))
    full = lambda shape: pl.BlockSpec(shape, lambda b, d, i: (0,) * len(shape))
    return pl.pallas_call(
        _rwkv_kernel,
        grid=(bsz, 2, nb),
        in_specs=[tok(HEAD_W)] * 5 + [tok(LANE),
                  pl.BlockSpec((1, 1, HEAD_W), lambda b, d, i: (d, 0, 0)),
                  pl.BlockSpec((1, LANE, HEAD_W), lambda b, d, i: (d, 0, 0)),
                  full((2, CHUNK, LANE)), full((2, CHUNK, LANE)),
                  full((CHUNK, LANE)), full((CHUNK, LANE))],
        out_specs=pl.BlockSpec((1, 1, BLK, HEAD_W), lambda b, d, i: (d, b, blk(b, d, i), 0)),
        out_shape=jax.ShapeDtypeStruct((2, bsz, t, HEAD_W), F32),
        scratch_shapes=[pltpu.VMEM((RWKV_H, RWKV_N, RWKV_N), F32)],
        compiler_params=_cparams(("arbitrary", "arbitrary", "arbitrary")),
        name="rwkv_scan",
    )(r, k2, v, kk, ab, tlw, w0, wdec, *[jnp.concatenate([m, m], axis=-1) for m in (tri, stri, bd16, eye)])


def _rope(x, c, sa, sb):
    w = x.shape[1]
    return x * c + pltpu.roll(x, w - ROPE_AXIS_DIM // 2, 1) * sa + pltpu.roll(x, ROPE_AXIS_DIM // 2, 1) * sb


def _att_prep_kernel(u_ref, c_ref, sa_ref, sb_ref, qg_ref, kg_ref, bd_ref, q_ref, kt_ref, v_ref):
    u = u_ref[0]
    bd = bd_ref[...].astype(BF16)
    c, sa, sb = c_ref[...], sa_ref[...], sb_ref[...]
    c2 = jnp.concatenate([c, c], axis=1)
    sa2 = jnp.concatenate([sa, sa], axis=1)
    sb2 = jnp.concatenate([sb, sb], axis=1)
    q = u[:, 0:256]
    k = u[:, 256:384]
    qms = _mm_mask_rhs(q * q, bd) * (1.0 / ATT_HD)
    kms = _mm_mask_rhs(k * k, bd[:LANE, :LANE]) * (1.0 / ATT_HD)
    qn = q * lax.rsqrt(qms + EPS) * qg_ref[...]
    kn = k * lax.rsqrt(kms + EPS) * kg_ref[...]
    q_ref[0] = _rope(qn, c2, sa2, sb2) * (ATT_HD ** -0.5)
    knt = _rope(kn, c, sa, sb).T
    kt_ref[0, 0] = knt[:ATT_HD].astype(BF16)
    kt_ref[0, 1] = knt[ATT_HD:].astype(BF16)
    v_ref[0, 0] = u[:, 384:448].astype(BF16)
    v_ref[0, 1] = u[:, 448:512].astype(BF16)


def _att_prep(u, c, sa, sb, qg, kg, bd64):
    bsz, t, _ = u.shape
    nb = t // BLK
    full = lambda shape: pl.BlockSpec(shape, lambda b, i: (0,) * len(shape))
    tab = pl.BlockSpec((BLK, LANE), lambda b, i: (i, 0))
    return pl.pallas_call(
        _att_prep_kernel,
        grid=(bsz, nb),
        in_specs=[pl.BlockSpec((1, BLK, ATT_W), lambda b, i: (b, i, 0)), tab, tab, tab,
                  full((1, HEAD_W)), full((1, LANE)), full((HEAD_W, HEAD_W))],
        out_specs=[pl.BlockSpec((1, BLK, HEAD_W), lambda b, i: (b, i, 0)),
                   pl.BlockSpec((1, ATT_HKV, ATT_HD, BLK), lambda b, i: (b, 0, 0, i)),
                   pl.BlockSpec((1, ATT_HKV, BLK, ATT_HD), lambda b, i: (b, 0, i, 0))],
        out_shape=[jax.ShapeDtypeStruct((bsz, t, HEAD_W), F32),
                   jax.ShapeDtypeStruct((bsz, ATT_HKV, ATT_HD, t), BF16),
                   jax.ShapeDtypeStruct((bsz, ATT_HKV, t, ATT_HD), BF16)],
        compiler_params=_cparams(("arbitrary", "arbitrary")),
        name="att_prep",
    )(u, c, sa, sb, qg, kg, bd64)


def _att_kernel(q_ref, kt_ref, v_ref, o_ref, *, nbc, n_ctx):
    i = pl.program_id(2)

    def attend(n_keys):
        q = q_ref[0]
        for h in range(ATT_HQ // ATT_HKV):
            for r in range(BLK // ATT_ROWS):
                rows = slice(r * ATT_ROWS, (r + 1) * ATT_ROWS)
                qs = q[rows, h * ATT_HD:(h + 1) * ATT_HD].astype(BF16)
                s = jnp.dot(qs, kt_ref[0, 0, :, :n_keys], preferred_element_type=F32)
                e = jnp.exp(s - jnp.max(s, axis=-1, keepdims=True))
                l = jnp.sum(e, axis=-1, keepdims=True)
                o = jnp.dot(e.astype(BF16), v_ref[0, 0, :n_keys, :], preferred_element_type=F32) / l
                o_ref[0, rows, h * ATT_HD:(h + 1) * ATT_HD] = o

    @pl.when(i < nbc)
    def _():
        attend(n_ctx)

    @pl.when(i >= nbc)
    def _():
        attend(kt_ref.shape[3])


def _attention(q, kt, v, nbc, n_ctx):
    bsz, t, _ = q.shape
    nb = t // BLK
    return pl.pallas_call(
        functools.partial(_att_kernel, nbc=nbc, n_ctx=n_ctx),
        grid=(bsz, ATT_HKV, nb),
        in_specs=[pl.BlockSpec((1, BLK, LANE), lambda b, g, i: (b, i, g)),
                  pl.BlockSpec((1, 1, ATT_HD, t), lambda b, g, i: (b, g, 0, 0)),
                  pl.BlockSpec((1, 1, t, ATT_HD), lambda b, g, i: (b, g, 0, 0))],
        out_specs=pl.BlockSpec((1, BLK, LANE), lambda b, g, i: (b, i, g)),
        out_shape=jax.ShapeDtypeStruct((bsz, t, HEAD_W), F32),
        compiler_params=_cparams(("arbitrary", "arbitrary", "arbitrary")),
        name="attention",
    )(q, kt, v)


def _out_kernel(h_ref, mod_ref, glaf_ref, glab_ref, glag_ref, ssdf_ref, ssdb_ref, xs_ref, z_ref,
                rwf_ref, rwb_ref, r_ref, k_ref, v_ref, g_ref, att_ref,
                glan_ref, ssd_d_ref, ssdn_ref, rk_ref, lng_ref, lnb_ref, bd_ref, w_ref, o_ref):
    bd = bd_ref[...].astype(BF16)
    seg_mean = lambda x: _mm_mask_rhs(x, bd) * (1.0 / 64.0)
    o = glaf_ref[0, 0] + glab_ref[0, 0]
    y_gla = o * lax.rsqrt(seg_mean(o * o) + EPS) * glan_ref[...] * _silu(glag_ref[0])
    y = ssdf_ref[0, 0] + ssdb_ref[0, 0] + ssd_d_ref[...] * xs_ref[0, 0]
    y = y * _silu(z_ref[0])
    y_ssd = y * lax.rsqrt(jnp.mean(y * y, axis=-1, keepdims=True) + EPS) * ssdn_ref[...]
    y = rwf_ref[0, 0] + rwb_ref[0, 0]
    mu = seg_mean(y)
    yc = y - mu
    var = seg_mean(yc * yc)
    yn = yc * lax.rsqrt(var + RWKV_GN_EPS) * lng_ref[...] + lnb_ref[...]
    v = v_ref[0]
    bonus = _mm_mask_rhs(r_ref[0] * k_ref[0] * rk_ref[...], bd) * v
    y_rw = (yn + bonus) * g_ref[0]
    w = w_ref[...]
    proj = (_mm(y_gla, w[0:256]) + _mm(y_ssd, w[256:512])) + (_mm(y_rw, w[512:768]) + _mm(att_ref[0], w[768:1024]))
    o_ref[0] = h_ref[0] + mod_ref[0][5:6] * proj


def _mix_out(h, mod, gla_o, u_gla, ssd_y, ssd_xs, u_ssm, rw_y, r, k2, v, g, att_o,
             gla_n, ssd_d, ssd_n, r_k, ln_g, ln_b, bd64, w_out, nbc):
    bsz, t, d = h.shape
    tok = lambda w_, col=0: pl.BlockSpec((1, BLK, w_), lambda b, i: (b, i, col))
    dirn = lambda dd: pl.BlockSpec((1, 1, BLK, HEAD_W), lambda b, i: (dd, b, i, 0))
    full = lambda shape: pl.BlockSpec(shape, lambda b, i: (0,) * len(shape))
    vec = full((1, HEAD_W))
    return pl.pallas_call(
        _out_kernel,
        grid=(bsz, t // BLK),
        in_specs=[tok(d), pl.BlockSpec((1, N_MOD, d), lambda b, i: (_mod_row(b, i, nbc), 0, 0)),
                  dirn(0), dirn(1), tok(HEAD_W, 2),
                  dirn(0), dirn(1), dirn(0), tok(HEAD_W, 0),
                  dirn(0), dirn(1), tok(HEAD_W), tok(HEAD_W), tok(HEAD_W), tok(HEAD_W), tok(HEAD_W),
                  vec, vec, vec, vec, vec, vec, full((HEAD_W, HEAD_W)), full((d, d))],
        out_specs=tok(d),
        out_shape=jax.ShapeDtypeStruct(h.shape, F32),
        compiler_params=_cparams(("arbitrary", "arbitrary")),
        name="mix_out",
    )(h, mod, gla_o, gla_o, u_gla, ssd_y, ssd_y, ssd_xs, u_ssm, rw_y, rw_y, r, k2, v, g, att_o,
      gla_n, ssd_d, ssd_n, r_k, ln_g, ln_b, bd64, w_out)


def _rope_tables(n_ctx, n_lat):
    rows = n_lat // GRID_W
    row = jnp.repeat(jnp.arange(rows, dtype=F32), GRID_W)
    col = jnp.tile(jnp.arange(GRID_W, dtype=F32), rows)
    inv = ROPE_THETA ** (-jnp.arange(0, ROPE_AXIS_DIM, 2, dtype=F32) / ROPE_AXIS_DIM)
    ang = jnp.stack([row[:, None] * inv, col[:, None] * inv], axis=1)
    cos, sin = jnp.cos(ang), jnp.sin(ang)
    zero = jnp.zeros_like(sin)
    c = jnp.concatenate([cos, cos], axis=-1).reshape(n_lat, ATT_HD)
    sa = jnp.concatenate([-sin, zero], axis=-1).reshape(n_lat, ATT_HD)
    sb = jnp.concatenate([zero, sin], axis=-1).reshape(n_lat, ATT_HD)
    pad = lambda x, fill: jnp.concatenate([jnp.full((n_ctx, ATT_HD), fill, F32), x], axis=0)
    two = lambda x: jnp.concatenate([x, x], axis=1)
    return two(pad(c, 1.0)), two(pad(sa, 0.0)), two(pad(sb, 0.0))


def _masks():
    t = jnp.arange(CHUNK)
    lower = (t[None, :] <= t[:, None]).astype(F32)
    slower = (t[None, :] < t[:, None]).astype(F32)
    tri = jnp.stack([lower, lower.T])
    stri = jnp.stack([slower, slower.T])
    bd16 = (t[None, :] // SUB == t[:, None] // SUB).astype(F32)
    eye = jnp.eye(CHUNK, dtype=F32)
    c = jnp.arange(HEAD_W)
    bd64 = (c[None, :] // 64 == c[:, None] // 64).astype(F32)
    expand = (jnp.arange(LANE)[:, None] == c[None, :] // 64).astype(F32)
    return tri, stri, bd16, eye, bd64, expand


def _pad_cols(x, width):
    return jnp.pad(x, [(0, 0)] * (x.ndim - 1) + [(0, width - x.shape[-1])])


def _pad_rows(x, height, top=0):
    return jnp.pad(x, [(0, 0)] * (x.ndim - 2) + [(top, height - top - x.shape[-2]), (0, 0)])


def _rep(x, n):
    return jnp.repeat(x, n, axis=-1)


def kernel(x, c, ctx, c_ctx, norm_g, w_mod, b_mod, ffn_in, ffn_out, w_in, w_out, gla_w_dec, gla_b_dec, gla_norm, ssm_conv_w, ssm_conv_b, ssm_dt_bias, ssm_a_log, ssm_d, ssm_norm, rwkv_shift_mu, rwkv_w0, rwkv_w_dec, rwkv_a0, rwkv_w_a, rwkv_w_g, rwkv_k_k, rwkv_k_a, rwkv_r_k, rwkv_ln_g, rwkv_ln_b, att_q_norm, att_k_norm):
    bsz, n_lat, d = x.shape
    n_ctx = ctx.shape[1]
    depth = w_mod.shape[0]
    assert bsz == 4 and n_ctx % BLK == 0 and n_lat % BLK == 0 and n_lat % GRID_W == 0
    nbc = n_ctx // BLK
    tri, stri, bd16, eye, bd64, expand = _masks()
    rope_c, rope_sa, rope_sb = _rope_tables(n_ctx, n_lat)

    cvec = jnp.concatenate([c, c_ctx[None], jnp.zeros((8 - bsz - 1, d), F32)], axis=0)
    mod_all = _compute_mod(cvec, w_mod, b_mod).reshape(depth, 8, N_MOD, d)

    h = jnp.concatenate([ctx, x], axis=1)
    for l in range(depth):
        mod = mod_all[l]
        w1 = ffn_in[l].astype(BF16)
        w2 = ffn_out[l].astype(BF16)
        wi = w_in[l]
        o_ssm = 784
        o_rw = o_ssm + 772
        o_att = o_rw + 1024
        w_proj = jnp.concatenate([_pad_cols(wi[:, :o_ssm], GLA_W), _pad_cols(wi[:, o_ssm:o_rw], SSM_W),
                                  wi[:, o_rw:o_att], wi[:, o_att:]], axis=1).astype(BF16)
        gla_wd = _pad_rows(gla_w_dec[l], LANE)
        gla_bd = gla_b_dec[l][:, None, :]
        ssm_dtb = _rep(ssm_dt_bias[l], SSM_P)[:, None, :]
        ssm_an = _rep(-jnp.exp(ssm_a_log[l]), SSM_P)[:, None, :]
        rw_wd = _pad_rows(rwkv_w_dec[l], LANE)
        rw_wa = _pad_rows(rwkv_w_a[l], LANE, top=64)
        rw_w0 = rwkv_w0[l][:, None, :]

        h = _ffn(h, mod, norm_g[l, 0][None], w1[0], w2[0], 0, nbc)
        u_gla, u_ssm, u_rw, u_att = _inproj(h, mod, norm_g[l, 1][None], w_proj, nbc)

        gla_o = _gla_scan(u_gla, gla_wd, gla_bd, tri, nbc)
        ssd_y, ssd_xs = _ssd_scan(u_ssm, ssm_conv_w[l], ssm_conv_b[l][None], ssm_dtb, ssm_an, expand, tri, nbc)
        r, k2, v, g, kk, ab, tlw = _rwkv_prep(u_rw, rwkv_shift_mu[l], rwkv_a0[l][None], rw_wa, rwkv_w_g[l],
                                              rwkv_k_k[l][None], rwkv_k_a[l][None], bd64, nbc)
        rw_y = _rwkv_scan(r, k2, v, kk, ab, tlw, rw_w0, rw_wd, tri, stri, bd16, eye, nbc)
        qn, kt, vt = _att_prep(u_att, rope_c, rope_sa, rope_sb, jnp.tile(att_q_norm[l], ATT_HQ)[None],
                               jnp.tile(att_k_norm[l], ATT_HKV)[None], bd64)
        att_o = _attention(qn, kt, vt, nbc, n_ctx)

        h = _mix_out(h, mod, gla_o, u_gla, ssd_y, ssd_xs, u_ssm, rw_y, r, k2, v, g, att_o,
                     jnp.tile(gla_norm[l], GLA_H)[None], _rep(ssm_d[l], SSM_P)[None], ssm_norm[l][None],
                     rwkv_r_k[l].reshape(1, HEAD_W), rwkv_ln_g[l][None], rwkv_ln_b[l][None], bd64,
                     w_out[l].astype(BF16), nbc)
        h = _ffn(h, mod, norm_g[l, 2][None], w1[1], w2[1], 6, nbc)
    return h[:, n_ctx:]
```

```python
import functools
import math

import jax
import jax.numpy as jnp
from jax import lax
from jax.experimental import pallas as pl
from jax.experimental.pallas import tpu as pltpu

F32 = jnp.float32
BF16 = jnp.bfloat16

N_MOD = 9
EPS = 1e-6
GLA_H, GLA_DK, GLA_DV, GLA_LORA, GLA_TAU = 4, 32, 64, 16, 16.0
SSM_H, SSM_P, SSM_G, SSM_N = 4, 64, 2, 64
RWKV_H, RWKV_N, RWKV_GN_EPS = 4, 64, 64e-5
ATT_HQ, ATT_HKV, ATT_HD = 4, 2, 64
GRID_W = 64
ROPE_THETA = 10000.0
ROPE_AXIS_DIM = ATT_HD // 2

CHUNK = 64
SUB = 16
BLK = 256
NSUB = BLK // CHUNK
ATT_ROWS = 128
LANE = 128
HEAD_W = 256
VMEM_LIMIT = 56 * 1024 * 1024

GLA_W = 896
SSM_W = 896
RWKV_W = 1024
ATT_W = 512


def _cparams(sem):
    return pltpu.CompilerParams(dimension_semantics=sem, vmem_limit_bytes=VMEM_LIMIT)


def _mm(a, b):
    return jnp.dot(a.astype(BF16), b.astype(BF16), preferred_element_type=F32)


def _dg(a, b, dims):
    return lax.dot_general(a, b, (dims, ((), ())), preferred_element_type=F32)


_NN = ((1,), (0,))
_NT = ((1,), (1,))
_TN = ((0,), (0,))


def _split2(x):
    hi = x.astype(BF16)
    lo = (x - hi.astype(F32)).astype(BF16)
    return hi, lo


def _split3(x):
    hi = x.astype(BF16)
    r1 = x - hi.astype(F32)
    mid = r1.astype(BF16)
    lo = (r1 - mid.astype(F32)).astype(BF16)
    return hi, mid, lo


def _mm3(a, b, dims=_NN):
    ah, al = _split2(a)
    bh, bl = _split2(b)
    return _dg(ah, bh, dims) + (_dg(ah, bl, dims) + _dg(al, bh, dims))


def _mm3s(a, b, dims=_NN):
    return _dg(a[0], b[0], dims) + (_dg(a[0], b[1], dims) + _dg(a[1], b[0], dims))


_parts = _split2


def _kdot(a, b):
    ah, al = a
    bh, bl = b
    return _dg(jnp.concatenate([ah, al], axis=1), jnp.concatenate([bh, bl, bh, bl], axis=0), _NN)


def _mm_mask_lhs(mask_bf16, x, dims=_NN):
    hi, mid, lo = _split3(x)
    return _dg(mask_bf16, hi, dims) + (_dg(mask_bf16, mid, dims) + _dg(mask_bf16, lo, dims))


def _mm_mask_rhs(x, mask_bf16, dims=_NN):
    hi, mid, lo = _split3(x)
    return _dg(hi, mask_bf16, dims) + (_dg(mid, mask_bf16, dims) + _dg(lo, mask_bf16, dims))


def _seg_sum(x, mask_bf16):
    hi, lo = _split2(x)
    return _dg(hi, mask_bf16, _NN) + _dg(lo, mask_bf16, _NN)


def _sigmoid(x):
    return jax.nn.sigmoid(x)


def _silu(x):
    return x * jax.nn.sigmoid(x)


def _softplus(x):
    return jnp.maximum(x, 0.0) + jnp.log1p(jnp.exp(-jnp.abs(x)))


def _modulate(x, g, shift, scale):
    ms = jnp.mean(x * x, axis=-1, keepdims=True)
    return (x * lax.rsqrt(ms + EPS) * g) * (1.0 + scale) + shift


def _block_of_step(d, i, nbc, nb):
    back = jnp.where(i < nbc, nbc - 1 - i, nb + nbc - 1 - i)
    return jnp.where(d == 0, i, back)


def _mod_row(b, i, nbc):
    return jnp.where(i < nbc, 4, b)


def _mod_kernel(c_ref, w_ref, b_ref, o_ref):
    o_ref[0] = _mm(_silu(c_ref[...]), w_ref[0]) + b_ref[0]


def _compute_mod(cvec, w_mod, b_mod):
    depth, d, n = w_mod.shape
    tn = 1024
    return pl.pallas_call(
        _mod_kernel,
        grid=(depth, n // tn),
        in_specs=[pl.BlockSpec((8, d), lambda l, j: (0, 0)),
                  pl.BlockSpec((1, d, tn), lambda l, j: (l, 0, j)),
                  pl.BlockSpec((1, 1, tn), lambda l, j: (l, 0, j))],
        out_specs=pl.BlockSpec((1, 8, tn), lambda l, j: (l, 0, j)),
        out_shape=jax.ShapeDtypeStruct((depth, 8, n), F32),
        compiler_params=_cparams(("arbitrary", "arbitrary")),
        name="adaln_mod",
    )(cvec, w_mod, b_mod.reshape(depth, 1, n))


def _ffn_math(x, m, g, w1_ref, w2_ref, i0, d_ff):
    hm = _modulate(x, g, m[i0:i0 + 1], m[i0 + 1:i0 + 2]).astype(BF16)
    ab = jnp.dot(hm, w1_ref[...], preferred_element_type=F32)
    a = ab[:, :d_ff]
    act = (_silu(a) * ab[:, d_ff:]).astype(BF16)
    y = jnp.dot(act, w2_ref[...], preferred_element_type=F32)
    return x + (0.5 * m[i0 + 2:i0 + 3]) * y


def _ffn_kernel(h_ref, mod_ref, g_ref, w1_ref, w2_ref, o_ref, *, i0, d_ff):
    o_ref[0] = _ffn_math(h_ref[0], mod_ref[0], g_ref[...], w1_ref, w2_ref, i0, d_ff)


def _ffn_first_kernel(ctx_ref, x_ref, mod_ref, g_ref, w1_ref, w2_ref, o_ref, *, i0, d_ff, nbc):
    x = jnp.where(pl.program_id(1) < nbc, ctx_ref[0], x_ref[0])
    o_ref[0] = _ffn_math(x, mod_ref[0], g_ref[...], w1_ref, w2_ref, i0, d_ff)


def _ffn_last_kernel(h_ref, mod_ref, g_ref, w1_ref, w2_ref, o_ref, *, i0, d_ff, nbc):
    @pl.when(pl.program_id(1) >= nbc)
    def _():
        o_ref[0] = _ffn_math(h_ref[0], mod_ref[0], g_ref[...], w1_ref, w2_ref, i0, d_ff)


def _mod_spec(layer, nbc, d):
    return pl.BlockSpec((None, 1, N_MOD, d), lambda b, i: (layer, _mod_row(b, i, nbc), 0, 0))


def _ffn(h, mod, g, w1, w2, layer, which, nbc, ctx=None, latent_only=False):
    bsz, _, d = h.shape
    d_ff = w2.shape[2]
    t = h.shape[1] + (0 if ctx is None else ctx.shape[1])
    tok = pl.BlockSpec((1, BLK, d), lambda b, i: (b, i, 0))
    lat = pl.BlockSpec((1, BLK, d), lambda b, i: (b, jnp.maximum(i - nbc, 0), 0))
    common = [_mod_spec(layer, nbc, d),
              pl.BlockSpec((1, d), lambda b, i: (0, 0)),
              pl.BlockSpec((None, None, d, 2 * d_ff), lambda b, i: (layer, which, 0, 0)),
              pl.BlockSpec((None, None, d_ff, d), lambda b, i: (layer, which, 0, 0))]
    if ctx is not None:
        body = functools.partial(_ffn_first_kernel, i0=6 * which, d_ff=d_ff, nbc=nbc)
        ins = [pl.BlockSpec((1, BLK, d), lambda b, i: (b, jnp.minimum(i, nbc - 1), 0)), lat]
        args = (ctx, h)
    else:
        body = functools.partial(_ffn_last_kernel if latent_only else _ffn_kernel, i0=6 * which, d_ff=d_ff,
                                 **({"nbc": nbc} if latent_only else {}))
        ins = [tok]
        args = (h,)
    return pl.pallas_call(
        body,
        grid=(bsz, t // BLK),
        in_specs=ins + common,
        out_specs=lat if latent_only else tok,
        out_shape=jax.ShapeDtypeStruct((bsz, t - nbc * BLK if latent_only else t, d), F32),
        compiler_params=_cparams(("arbitrary", "arbitrary")),
        name="ffn",
    )(*args, mod, g, w1, w2)


def _inproj_kernel(h_ref, mod_ref, g_ref, w_ref, gla_ref, ssm_ref, rwkv_ref, att_ref):
    m = mod_ref[0]
    hm = _modulate(h_ref[0], g_ref[...], m[3:4], m[4:5]).astype(BF16)
    u = jnp.dot(hm, w_ref[...], preferred_element_type=F32)
    gla_ref[0] = u[:, :GLA_W]
    ssm_ref[0] = u[:, GLA_W:GLA_W + SSM_W]
    rwkv_ref[0] = u[:, GLA_W + SSM_W:GLA_W + SSM_W + RWKV_W]
    att_ref[0] = u[:, GLA_W + SSM_W + RWKV_W:]


def _inproj(h, mod, g, w, layer, nbc):
    bsz, t, d = h.shape
    widths = (GLA_W, SSM_W, RWKV_W, ATT_W)
    return pl.pallas_call(
        _inproj_kernel,
        grid=(bsz, t // BLK),
        in_specs=[pl.BlockSpec((1, BLK, d), lambda b, i: (b, i, 0)),
                  _mod_spec(layer, nbc, d),
                  pl.BlockSpec((1, d), lambda b, i: (0, 0)),
                  pl.BlockSpec((None, d, sum(widths)), lambda b, i: (layer, 0, 0))],
        out_specs=[pl.BlockSpec((1, BLK, w_), lambda b, i: (b, i, 0)) for w_ in widths],
        out_shape=[jax.ShapeDtypeStruct((bsz, t, w_), F32) for w_ in widths],
        compiler_params=_cparams(("arbitrary", "arbitrary")),
        name="inproj",
    )(h, mod, g, w)


def _gla_kernel(u_ref, wdec_ref, bdec_ref, tri_ref, o_ref, s_ref):
    d = pl.program_id(1)
    i = pl.program_id(2)

    @pl.when(i == 0)
    def _():
        s_ref[...] = jnp.zeros_like(s_ref)

    tri = tri_ref[d]
    tri_b = tri.astype(BF16)
    wdec = wdec_ref[0].astype(BF16)
    bdec = bdec_ref[0]

    rows_of, qh, kih, keh, vh, dec = [], [], [], [], [], []
    for j in range(NSUB):
        jj = jnp.where(d == 0, j, NSUB - 1 - j)
        rows = pl.ds(pl.multiple_of(jj * CHUNK, CHUNK), CHUNK)
        rows_of.append(rows)
        u = u_ref[0, rows, :]
        q = u[:, 0:128] * (GLA_DK ** -0.5)
        k = u[:, 128:256]
        v = u[:, 256:512]
        z = _mm(u[:, 768:896], wdec) + bdec
        la = -_softplus(-z) * (1.0 / GLA_TAU)
        bc = _mm_mask_lhs(tri_b, la)
        bt = jnp.sum(la, axis=0, keepdims=True)
        qd = q * jnp.exp(bc)
        ki = k * jnp.exp(-bc)
        ke = k * jnp.exp(bt - bc)
        dec_j = jnp.exp(bt)
        for h in range(GLA_H):
            ks = slice(GLA_DK * h, GLA_DK * (h + 1))
            qh.append(qd[:, ks].astype(BF16))
            kih.append(ki[:, ks].astype(BF16))
            keh.append(ke[:, ks].astype(BF16))
            vh.append(v[:, GLA_DV * h:GLA_DV * (h + 1)].astype(BF16))
            dec.append(dec_j[:, ks])
    att = [_dg(a, b, _NT) * tri for a, b in zip(qh, kih)]
    o_intra = [_mm(a, b) for a, b in zip(att, vh)]
    kv = [_dg(a, b, _TN) for a, b in zip(vh, keh)]
    st = [s_ref[h] for h in range(GLA_H)]
    s_prev = []
    for j in range(NSUB):
        for h in range(GLA_H):
            n = j * GLA_H + h
            s_prev.append(st[h].astype(BF16))
            st[h] = st[h] * dec[n] + kv[n]
    for h in range(GLA_H):
        s_ref[h] = st[h]
    for j in range(NSUB):
        for h in range(GLA_H):
            n = j * GLA_H + h
            o_ref[0, 0, rows_of[j], GLA_DV * h:GLA_DV * (h + 1)] = o_intra[n] + _dg(qh[n], s_prev[n], _NT)


def _gla_scan(u, wdec, bdec, tri, nbc):
    bsz, t, _ = u.shape
    nb = t // BLK
    blk = lambda b, d, i: _block_of_step(d, i, nbc, nb)
    return pl.pallas_call(
        _gla_kernel,
        grid=(bsz, 2, nb),
        in_specs=[pl.BlockSpec((1, BLK, GLA_W), lambda b, d, i: (b, blk(b, d, i), 0)),
                  pl.BlockSpec((1, LANE, LANE), lambda b, d, i: (d, 0, 0)),
                  pl.BlockSpec((1, 1, LANE), lambda b, d, i: (d, 0, 0)),
                  pl.BlockSpec((2, CHUNK, CHUNK), lambda b, d, i: (0, 0, 0))],
        out_specs=pl.BlockSpec((1, 1, BLK, HEAD_W), lambda b, d, i: (d, b, blk(b, d, i), 0)),
        out_shape=jax.ShapeDtypeStruct((2, bsz, t, HEAD_W), F32),
        scratch_shapes=[pltpu.VMEM((GLA_H, GLA_DV, GLA_DK), F32)],
        compiler_params=_cparams(("arbitrary", "arbitrary", "arbitrary")),
        name="gla_scan",
    )(u, wdec, bdec, tri)


def _shifted(x, prev_row, next_row):
    n = x.shape[0]
    row = lax.broadcasted_iota(jnp.int32, x.shape, 0)
    xm1 = jnp.where(row == 0, prev_row, pltpu.roll(x, 1, 0))
    xp1 = jnp.where(row == n - 1, next_row, pltpu.roll(x, n - 1, 0))
    return xm1, xp1


def _halo_specs(width, blk_fn):
    per = BLK // 8

    def prev_map(*idx):
        b, blk = blk_fn(*idx)
        return (b, jnp.maximum(blk * per - 1, 0), 0)

    def next_map(nrow8):
        def f(*idx):
            b, blk = blk_fn(*idx)
            return (b, jnp.minimum((blk + 1) * per, nrow8 - 1), 0)
        return f

    return (lambda: pl.BlockSpec((1, 8, width), prev_map),
            lambda nrow8: pl.BlockSpec((1, 8, width), next_map(nrow8)))


def _stream_edges(blk, nbc, nb):
    has_prev = jnp.logical_and(blk != 0, blk != nbc)
    has_next = jnp.logical_and(blk != nbc - 1, blk != nb - 1)
    return has_prev.astype(F32), has_next.astype(F32)


def _ssd_kernel(u_ref, up_ref, un_ref, cw_ref, cb_ref, dtb_ref, aneg_ref, exp_ref, tri_ref,
                y_ref, xs_ref, h_ref, xdt_s, bm_s, cm_s, da_s, *, nbc, nb):
    d = pl.program_id(1)
    i = pl.program_id(2)
    blk = _block_of_step(d, i, nbc, nb)

    @pl.when(i == 0)
    def _():
        h_ref[...] = jnp.zeros_like(h_ref)

    has_prev, has_next = _stream_edges(blk, nbc, nb)
    u = u_ref[0]
    xbc = u[:, 256:768]
    prev_row = up_ref[0, 7:8, 256:768] * has_prev
    next_row = un_ref[0, 0:1, 256:768] * has_next
    xm1, xp1 = _shifted(xbc, prev_row, next_row)
    cw = cw_ref[...]
    act = _silu(cw[0:1] * xm1 + cw[1:2] * xbc + cw[2:3] * xp1 + cb_ref[...])
    xs = act[:, 0:256]
    xs_ref[0, 0] = xs
    dt = _softplus(_mm_mask_rhs(u[:, 768:896], exp_ref[...].astype(BF16)) + dtb_ref[0])
    xdt_s[...] = xs * dt
    da_s[...] = dt * aneg_ref[0]
    bm_s[...] = act[:, 256:384]
    cm_s[...] = act[:, 384:512]

    tri = tri_ref[d]
    tri_b = tri.astype(BF16)

    per_g = SSM_H // SSM_G
    per_l = LANE // SSM_P
    rows_of, cgs, bgs, bgt, xdh, xh, cum_col, cum_row, dec = [], [], [], [], [], [], [], [], []
    for j in range(NSUB):
        jj = jnp.where(d == 0, j, NSUB - 1 - j)
        rows = pl.ds(pl.multiple_of(jj * CHUNK, CHUNK), CHUNK)
        rows_of.append(rows)
        da = da_s[rows, :]
        xdt = xdt_s[rows, :]
        bm = bm_s[rows, :]
        cm = cm_s[rows, :]
        cum = _mm_mask_lhs(tri_b, da)
        tot = jnp.sum(da, axis=0, keepdims=True)
        cum_t = [cum[:, LANE * c:LANE * (c + 1)].T for c in range(HEAD_W // LANE)]
        bm_t = bm.T.astype(BF16)
        for h in range(SSM_H):
            hs = slice(SSM_P * h, SSM_P * (h + 1))
            gs = slice(SSM_N * (h // per_g), SSM_N * (h // per_g + 1))
            cgs.append(cm[:, gs])
            bgs.append(bm[:, gs])
            bgt.append(bm_t[gs, :])
            xdh.append(xdt[:, hs])
            cum_row.append(cum_t[h // per_l][SSM_P * (h % per_l):SSM_P * (h % per_l) + 1, :])
            xh.append(xdt[:, hs].astype(BF16))
            cum_col.append(cum[:, hs])
            dec.append(tot[:, hs])
    cb = [_dg(c.astype(BF16), b.astype(BF16), _NT) for c, b in zip(cgs[::per_g], bgs[::per_g])]
    seg = [jnp.where(tri > 0, jnp.exp(c - r), 0.0) for c, r in zip(cum_col, cum_row)]
    y_intra = [_mm(cb[n // per_g] * s, x) for n, (s, x) in enumerate(zip(seg, xh))]
    contrib = [_dg(b, (x * jnp.exp(t - c)).astype(BF16), _NN) for b, t, c, x in zip(bgt, dec, cum_col, xdh)]
    hp = [h_ref[h] for h in range(SSM_H)]
    h_prev = []
    for j in range(NSUB):
        for h in range(SSM_H):
            n = j * SSM_H + h
            h_prev.append(hp[h])
            hp[h] = hp[h] * jnp.exp(dec[n]) + contrib[n]
    for h in range(SSM_H):
        h_ref[h] = hp[h]
    for j in range(NSUB):
        for h in range(SSM_H):
            n = j * SSM_H + h
            y_ref[0, 0, rows_of[j], SSM_P * h:SSM_P * (h + 1)] = (
                y_intra[n] + _mm(cgs[n] * jnp.exp(cum_col[n]), h_prev[n]))


def _ssd_scan(u, conv_w, conv_b, dtb, aneg, expand, tri, nbc):
    bsz, t, _ = u.shape
    nb = t // BLK
    blk = lambda b, d, i: _block_of_step(d, i, nbc, nb)
    prev_spec, next_spec = _halo_specs(SSM_W, lambda b, d, i: (b, blk(b, d, i)))
    full = lambda shape: pl.BlockSpec(shape, lambda b, d, i: (0,) * len(shape))
    return pl.pallas_call(
        functools.partial(_ssd_kernel, nbc=nbc, nb=nb),
        grid=(bsz, 2, nb),
        in_specs=[pl.BlockSpec((1, BLK, SSM_W), lambda b, d, i: (b, blk(b, d, i), 0)),
                  prev_spec(), next_spec(t // 8),
                  full((3, 512)), full((1, 512)),
                  pl.BlockSpec((1, 1, HEAD_W), lambda b, d, i: (d, 0, 0)),
                  pl.BlockSpec((1, 1, HEAD_W), lambda b, d, i: (d, 0, 0)),
                  full((LANE, HEAD_W)), full((2, CHUNK, CHUNK))],
        out_specs=[pl.BlockSpec((1, 1, BLK, HEAD_W), lambda b, d, i: (d, b, blk(b, d, i), 0))] * 2,
        out_shape=[jax.ShapeDtypeStruct((2, bsz, t, HEAD_W), F32)] * 2,
        scratch_shapes=[pltpu.VMEM((SSM_H, SSM_N, SSM_P), F32),
                        pltpu.VMEM((BLK, HEAD_W), F32), pltpu.VMEM((BLK, LANE), F32),
                        pltpu.VMEM((BLK, LANE), F32), pltpu.VMEM((BLK, HEAD_W), F32)],
        compiler_params=_cparams(("arbitrary", "arbitrary", "arbitrary")),
        name="ssd_scan",
    )(u, u, u, conv_w, conv_b, dtb, aneg, expand, tri)


def _rwkv_prep_kernel(u_ref, up_ref, un_ref, mu_ref, a0_ref, wa_ref, wg_ref, kk_ref, ka_ref, bd_ref,
                      r_ref, k_ref, v_ref, g_ref, kkn_ref, ab_ref, tlw_ref, *, nbc, nb):
    blk = pl.program_id(1)
    has_prev, has_next = _stream_edges(blk, nbc, nb)
    u = u_ref[0]
    xm1, xp1 = _shifted(u, up_ref[0, 7:8, :] * has_prev, un_ref[0, 0:1, :] * has_next)
    mu = mu_ref[...]
    u = u + mu[0:1] * (xm1 - u) + mu[1:2] * (xp1 - u)
    r = u[:, 0:256]
    k = u[:, 256:512]
    lwa = u[:, 768:896]
    a = _sigmoid(a0_ref[...] + _mm(lwa, wa_ref[...]))
    g = _mm(_sigmoid(u[:, 896:1024]), wg_ref[...])
    kk = k * kk_ref[...]
    ss = _seg_sum(kk * kk, bd_ref[...].astype(BF16))
    kk = kk / jnp.maximum(jnp.sqrt(ss), 1e-12)
    r_ref[0] = r
    k_ref[0] = k * (1.0 + (a - 1.0) * ka_ref[...])
    v_ref[0] = u[:, 512:768]
    g_ref[0] = g
    kkn_ref[0] = kk
    ab_ref[0] = kk * a
    tlw_ref[0] = jnp.tanh(lwa)


def _rwkv_prep(u, mu, a0, wa, wg, k_k, k_a, bd64, nbc):
    bsz, t, _ = u.shape
    nb = t // BLK
    prev_spec, next_spec = _halo_specs(RWKV_W, lambda b, i: (b, i))
    full = lambda shape: pl.BlockSpec(shape, lambda b, i: (0,) * len(shape))
    tok = lambda w_: pl.BlockSpec((1, BLK, w_), lambda b, i: (b, i, 0))
    return pl.pallas_call(
        functools.partial(_rwkv_prep_kernel, nbc=nbc, nb=nb),
        grid=(bsz, nb),
        in_specs=[tok(RWKV_W), prev_spec(), next_spec(t // 8),
                  full((2, RWKV_W)), full((1, HEAD_W)), full((LANE, HEAD_W)), full((LANE, HEAD_W)),
                  full((1, HEAD_W)), full((1, HEAD_W)), full((HEAD_W, HEAD_W))],
        out_specs=[tok(HEAD_W)] * 6 + [tok(LANE)],
        out_shape=[jax.ShapeDtypeStruct((bsz, t, HEAD_W), F32)] * 6
        + [jax.ShapeDtypeStruct((bsz, t, LANE), F32)],
        compiler_params=_cparams(("arbitrary", "arbitrary")),
        name="rwkv_prep",
    )(u, u, u, mu, a0, wa, wg, k_k, k_a, bd64)


def _rwkv_kernel(r_ref, k_ref, v_ref, kk_ref, ab_ref, tlw_ref, w0_ref, wdec_ref, tri_ref, stri_ref,
                 bd16_ref, eye_ref, y_ref, h_ref):
    d = pl.program_id(1)
    i = pl.program_id(2)

    @pl.when(i == 0)
    def _():
        h_ref[...] = jnp.zeros_like(h_ref)

    incl2 = tri_ref[d]
    strict2 = stri_ref[d]
    incl = incl2[:, :CHUNK]
    strict = strict2[:, :CHUNK]
    incl_b = incl.astype(BF16)
    bd16 = bd16_ref[...]
    eye = eye_ref[...]
    w0 = w0_ref[0]
    wdec = wdec_ref[0].astype(BF16)

    mask2 = jnp.concatenate([strict, incl], axis=0)
    stack = lambda a, b: jnp.concatenate([a, b], axis=0)
    heads = [slice(RWKV_N * h, RWKV_N * (h + 1)) for h in range(RWKV_H)]
    bf = lambda x: x.astype(BF16)

    rows_of, ar, rt, bt, kt, vh, bh, kh, pc = [], [], [], [], [], [], [], [], []
    for j in range(NSUB):
        jj = jnp.where(d == 0, j, NSUB - 1 - j)
        rows = pl.ds(pl.multiple_of(jj * CHUNK, CHUNK), CHUNK)
        rows_of.append(rows)
        wr = w0 + _mm(tlw_ref[0, rows, :], wdec)
        lw = -jnp.exp(-_softplus(-wr) - 0.5)
        gc = _mm_mask_lhs(incl_b, lw)
        tot = jnp.sum(lw, axis=0, keepdims=True)
        eng = jnp.exp(-gc)
        e_end = jnp.exp(tot - gc)
        abv = ab_ref[0, rows, :]
        k2 = k_ref[0, rows, :]
        at_j = bf(-kk_ref[0, rows, :] * jnp.exp(gc - lw))
        rt_f = r_ref[0, rows, :] * jnp.exp(gc)
        rt_j = bf(rt_f)
        bt_j = bf(abv * eng)
        kt_j = bf(k2 * eng)
        v_j = bf(v_ref[0, rows, :])
        bh_j = bf(abv * e_end)
        kh_j = bf(k2 * e_end)
        pc_j = jnp.exp(tot)
        for hs in heads:
            ar.append(stack(at_j[:, hs], rt_j[:, hs]))
            rt.append(rt_f[:, hs])
            bt.append(stack(bt_j[:, hs], bt_j[:, hs]))
            kt.append(kt_j[:, hs])
            vh.append(v_j[:, hs])
            bh.append(bh_j[:, hs])
            kh.append(kh_j[:, hs])
            pc.append(pc_j[:, hs])

    g1 = [_dg(a, b, _NT) for a, b in zip(ar, bt)]
    g2 = [_dg(a, b, _NT) for a, b in zip(ar, kt)]
    a_ab = [g[:CHUNK] * strict2 for g in g1]
    a_rb = [bf(g[CHUNK:, :CHUNK] * incl) for g in g1]
    avr = [_dg(bf(g * mask2), v, _NN) for g, v in zip(g2, vh)]
    khv = [_dg(v, k, _TN) for v, k in zip(vh, kh)]
    ad = [a * bd16 for a in a_ab]
    ee = [_parts(a - b) for a, b in zip(a_ab, ad)]
    q = [eye + a for a in ad]
    pw = [_kdot(x, x) for x in map(_parts, ad)]
    for _ in range(2):
        res = [_kdot(_parts(stack(a, x)), _parts(a)) for a, x in zip(pw, q)]
        pw = [r[:CHUNK] for r in res]
        q = [x + r[CHUNK:] for x, r in zip(q, res)]
    p = [x + _kdot(_parts(x), _parts(a)) for x, a in zip(q, pw)]
    p_s = [_parts(x) for x in p]
    f = [_parts(_kdot(x, e)) for x, e in zip(p_s, ee)]
    f2 = [_parts(_kdot(x, x)) for x in f]
    t1 = [x + _kdot(a, xs) for x, a, xs in zip(p, f, p_s)]
    tinv = [bf((x + _kdot(a, _parts(x)))[:, :CHUNK]) for x, a in zip(t1, f2)]
    z = [jnp.concatenate([a[:CHUNK], bf(c[:CHUNK])], axis=1) for a, c in zip(ar, avr)]
    tz = [bf(_dg(t, x, _NN)) for t, x in zip(tinv, z)]
    rz = [_dg(b, x, _NN) for b, x in zip(a_rb, tz)]
    w = [bf(_dg(t, b, _TN)) for t, b in zip(tinv, bh)]
    mn = [_dg(x, y, _TN) for x, y in zip(z, w)]
    qm = [bf(r + x[:, :CHUNK]) for r, x in zip(rt, rz)]
    y0 = [c[CHUNK:] + x[:, CHUNK:] for c, x in zip(avr, rz)]
    m1 = [bf(x[:CHUNK]) for x in mn]
    n1 = [x[CHUNK:] + e for x, e in zip(mn, khv)]

    ht = [h_ref[h] for h in range(RWKV_H)]
    for j in range(NSUB):
        for h in range(RWKV_H):
            n = j * RWKV_H + h
            hb = bf(ht[h])
            y_ref[0, 0, rows_of[j], heads[h]] = _dg(qm[n], hb, _NT) + y0[n]
            ht[h] = ht[h] * pc[n] + _dg(hb, m1[n], _NN) + n1[n]
    for h in range(RWKV_H):
        h_ref[h] = ht[h]


def _rwkv_scan(r, k2, v, kk, ab, tlw, w0, wdec, tri, stri, bd16, eye, nbc):
    bsz, t, _ = r.shape
    nb = t // BLK
    blk = lambda b, d, i: _block_of_step(d, i, nbc, nb)
    tok = lambda w_: pl.BlockSpec((1, BLK, w_), lambda b, d, i: (b, blk(b, d, i), 0))
    full = lambda shape: pl.BlockSpec(shape, lambda b, d, i: (0,) * len(shape))
    return pl.pallas_call(
        _rwkv_kernel,
        grid=(bsz, 2, nb),
        in_specs=[tok(HEAD_W)] * 5 + [tok(LANE),
                  pl.BlockSpec((1, 1, HEAD_W), lambda b, d, i: (d, 0, 0)),
                  pl.BlockSpec((1, LANE, HEAD_W), lambda b, d, i: (d, 0, 0)),
                  full((2, CHUNK, LANE)), full((2, CHUNK, LANE)),
                  full((CHUNK, LANE)), full((CHUNK, LANE))],
        out_specs=pl.BlockSpec((1, 1, BLK, HEAD_W), lambda b, d, i: (d, b, blk(b, d, i), 0)),
        out_shape=jax.ShapeDtypeStruct((2, bsz, t, HEAD_W), F32),
        scratch_shapes=[pltpu.VMEM((RWKV_H, RWKV_N, RWKV_N), F32)],
        compiler_params=_cparams(("arbitrary", "arbitrary", "arbitrary")),
        name="rwkv_scan",
    )(r, k2, v, kk, ab, tlw, w0, wdec, *[jnp.concatenate([m, m], axis=-1) for m in (tri, stri, bd16, eye)])


def _rope(x, c, sa, sb):
    w = x.shape[1]
    return x * c + pltpu.roll(x, w - ROPE_AXIS_DIM // 2, 1) * sa + pltpu.roll(x, ROPE_AXIS_DIM // 2, 1) * sb


def _att_prep_kernel(u_ref, c_ref, sa_ref, sb_ref, qg_ref, kg_ref, bd_ref, q_ref, kt_ref, v_ref):
    u = u_ref[0]
    bd = bd_ref[...].astype(BF16)
    c, sa, sb = c_ref[...], sa_ref[...], sb_ref[...]
    c2 = jnp.concatenate([c, c], axis=1)
    sa2 = jnp.concatenate([sa, sa], axis=1)
    sb2 = jnp.concatenate([sb, sb], axis=1)
    q = u[:, 0:256]
    k = u[:, 256:384]
    qms = _seg_sum(q * q, bd) * (1.0 / ATT_HD)
    kms = _seg_sum(k * k, bd[:LANE, :LANE]) * (1.0 / ATT_HD)
    qn = q * lax.rsqrt(qms + EPS) * qg_ref[...]
    kn = k * lax.rsqrt(kms + EPS) * kg_ref[...]
    q_ref[0] = _rope(qn, c2, sa2, sb2) * (ATT_HD ** -0.5)
    knt = _rope(kn, c, sa, sb).T
    kt_ref[0, 0] = knt[:ATT_HD].astype(BF16)
    kt_ref[0, 1] = knt[ATT_HD:].astype(BF16)
    v_ref[0, 0] = u[:, 384:448].astype(BF16)
    v_ref[0, 1] = u[:, 448:512].astype(BF16)


def _att_prep(u, c, sa, sb, qg, kg, bd64):
    bsz, t, _ = u.shape
    nb = t // BLK
    full = lambda shape: pl.BlockSpec(shape, lambda b, i: (0,) * len(shape))
    tab = pl.BlockSpec((BLK, LANE), lambda b, i: (i, 0))
    return pl.pallas_call(
        _att_prep_kernel,
        grid=(bsz, nb),
        in_specs=[pl.BlockSpec((1, BLK, ATT_W), lambda b, i: (b, i, 0)), tab, tab, tab,
                  full((1, HEAD_W)), full((1, LANE)), full((HEAD_W, HEAD_W))],
        out_specs=[pl.BlockSpec((1, BLK, HEAD_W), lambda b, i: (b, i, 0)),
                   pl.BlockSpec((1, ATT_HKV, ATT_HD, BLK), lambda b, i: (b, 0, 0, i)),
                   pl.BlockSpec((1, ATT_HKV, BLK, ATT_HD), lambda b, i: (b, 0, i, 0))],
        out_shape=[jax.ShapeDtypeStruct((bsz, t, HEAD_W), F32),
                   jax.ShapeDtypeStruct((bsz, ATT_HKV, ATT_HD, t), BF16),
                   jax.ShapeDtypeStruct((bsz, ATT_HKV, t, ATT_HD), BF16)],
        compiler_params=_cparams(("arbitrary", "arbitrary")),
        name="att_prep",
    )(u, c, sa, sb, qg, kg, bd64)


def _att_kernel(q_ref, kt_ref, v_ref, o_ref, *, nbc, n_ctx):
    i = pl.program_id(2)

    def attend(n_keys):
        q = q_ref[0]
        for h in range(ATT_HQ // ATT_HKV):
            for r in range(BLK // ATT_ROWS):
                rows = slice(r * ATT_ROWS, (r + 1) * ATT_ROWS)
                qs = q[rows, h * ATT_HD:(h + 1) * ATT_HD].astype(BF16)
                s = jnp.dot(qs, kt_ref[0, 0, :, :n_keys], preferred_element_type=F32)
                e = jnp.exp(s - jnp.max(s, axis=-1, keepdims=True))
                l = jnp.sum(e, axis=-1, keepdims=True)
                o = jnp.dot(e.astype(BF16), v_ref[0, 0, :n_keys, :], preferred_element_type=F32) / l
                o_ref[0, rows, h * ATT_HD:(h + 1) * ATT_HD] = o

    @pl.when(i < nbc)
    def _():
        attend(n_ctx)

    @pl.when(i >= nbc)
    def _():
        attend(kt_ref.shape[3])


def _attention(q, kt, v, nbc, n_ctx):
    bsz, t, _ = q.shape
    nb = t // BLK
    return pl.pallas_call(
        functools.partial(_att_kernel, nbc=nbc, n_ctx=n_ctx),
        grid=(bsz, ATT_HKV, nb),
        in_specs=[pl.BlockSpec((1, BLK, LANE), lambda b, g, i: (b, i, g)),
                  pl.BlockSpec((1, 1, ATT_HD, t), lambda b, g, i: (b, g, 0, 0)),
                  pl.BlockSpec((1, 1, t, ATT_HD), lambda b, g, i: (b, g, 0, 0))],
        out_specs=pl.BlockSpec((1, BLK, LANE), lambda b, g, i: (b, i, g)),
        out_shape=jax.ShapeDtypeStruct((bsz, t, HEAD_W), F32),
        compiler_params=_cparams(("arbitrary", "arbitrary", "arbitrary")),
        name="attention",
    )(q, kt, v)


def _out_kernel(h_ref, mod_ref, glaf_ref, glab_ref, glag_ref, ssdf_ref, ssdb_ref, xs_ref, z_ref,
                rwf_ref, rwb_ref, r_ref, k_ref, v_ref, g_ref, att_ref,
                glan_ref, ssd_d_ref, ssdn_ref, rk_ref, lng_ref, lnb_ref, bd_ref, w_ref, o_ref):
    bd = bd_ref[...].astype(BF16)
    seg_mean = lambda x: _seg_sum(x, bd) * (1.0 / 64.0)
    o = glaf_ref[0, 0] + glab_ref[0, 0]
    y_gla = o * lax.rsqrt(seg_mean(o * o) + EPS) * glan_ref[...] * _silu(glag_ref[0])
    y = ssdf_ref[0, 0] + ssdb_ref[0, 0] + ssd_d_ref[...] * xs_ref[0, 0]
    y = y * _silu(z_ref[0])
    y_ssd = y * lax.rsqrt(jnp.mean(y * y, axis=-1, keepdims=True) + EPS) * ssdn_ref[...]
    y = rwf_ref[0, 0] + rwb_ref[0, 0]
    mu = seg_mean(y)
    yc = y - mu
    var = seg_mean(yc * yc)
    yn = yc * lax.rsqrt(var + RWKV_GN_EPS) * lng_ref[...] + lnb_ref[...]
    v = v_ref[0]
    bonus = _seg_sum(r_ref[0] * k_ref[0] * rk_ref[...], bd) * v
    y_rw = (yn + bonus) * g_ref[0]
    w = w_ref[...]
    proj = (_mm(y_gla, w[0:256]) + _mm(y_ssd, w[256:512])) + (_mm(y_rw, w[512:768]) + _mm(att_ref[0], w[768:1024]))
    o_ref[0] = h_ref[0] + mod_ref[0][5:6] * proj


def _mix_out(h, mod, gla_o, u_gla, ssd_y, ssd_xs, u_ssm, rw_y, r, k2, v, g, att_o,
             gla_n, ssd_d, ssd_n, r_k, ln_g, ln_b, bd64, w_out, layer, nbc):
    bsz, t, d = h.shape
    tok = lambda w_, col=0: pl.BlockSpec((1, BLK, w_), lambda b, i: (b, i, col))
    dirn = lambda dd: pl.BlockSpec((1, 1, BLK, HEAD_W), lambda b, i: (dd, b, i, 0))
    full = lambda shape: pl.BlockSpec(shape, lambda b, i: (0,) * len(shape))
    vec = full((1, HEAD_W))
    return pl.pallas_call(
        _out_kernel,
        grid=(bsz, t // BLK),
        in_specs=[tok(d), _mod_spec(layer, nbc, d),
                  dirn(0), dirn(1), tok(HEAD_W, 2),
                  dirn(0), dirn(1), dirn(0), tok(HEAD_W, 0),
                  dirn(0), dirn(1), tok(HEAD_W), tok(HEAD_W), tok(HEAD_W), tok(HEAD_W), tok(HEAD_W),
                  vec, vec, vec, vec, vec, vec, full((HEAD_W, HEAD_W)),
                  pl.BlockSpec((None, d, d), lambda b, i: (layer, 0, 0))],
        out_specs=tok(d),
        out_shape=jax.ShapeDtypeStruct(h.shape, F32),
        compiler_params=_cparams(("arbitrary", "arbitrary")),
        name="mix_out",
    )(h, mod, gla_o, gla_o, u_gla, ssd_y, ssd_y, ssd_xs, u_ssm, rw_y, rw_y, r, k2, v, g, att_o,
      gla_n, ssd_d, ssd_n, r_k, ln_g, ln_b, bd64, w_out)


def _rope_tables(n_ctx, n_lat):
    rows = n_lat // GRID_W
    row = jnp.repeat(jnp.arange(rows, dtype=F32), GRID_W)
    col = jnp.tile(jnp.arange(GRID_W, dtype=F32), rows)
    inv = ROPE_THETA ** (-jnp.arange(0, ROPE_AXIS_DIM, 2, dtype=F32) / ROPE_AXIS_DIM)
    ang = jnp.stack([row[:, None] * inv, col[:, None] * inv], axis=1)
    cos, sin = jnp.cos(ang), jnp.sin(ang)
    zero = jnp.zeros_like(sin)
    c = jnp.concatenate([cos, cos], axis=-1).reshape(n_lat, ATT_HD)
    sa = jnp.concatenate([-sin, zero], axis=-1).reshape(n_lat, ATT_HD)
    sb = jnp.concatenate([zero, sin], axis=-1).reshape(n_lat, ATT_HD)
    pad = lambda x, fill: jnp.concatenate([jnp.full((n_ctx, ATT_HD), fill, F32), x], axis=0)
    two = lambda x: jnp.concatenate([x, x], axis=1)
    return two(pad(c, 1.0)), two(pad(sa, 0.0)), two(pad(sb, 0.0))


def _masks():
    t = jnp.arange(CHUNK)
    lower = (t[None, :] <= t[:, None]).astype(F32)
    slower = (t[None, :] < t[:, None]).astype(F32)
    tri = jnp.stack([lower, lower.T])
    stri = jnp.stack([slower, slower.T])
    bd16 = (t[None, :] // SUB == t[:, None] // SUB).astype(F32)
    eye = jnp.eye(CHUNK, dtype=F32)
    c = jnp.arange(HEAD_W)
    bd64 = (c[None, :] // 64 == c[:, None] // 64).astype(F32)
    expand = (jnp.arange(LANE)[:, None] == c[None, :] // 64).astype(F32)
    return tri, stri, bd16, eye, bd64, expand


def _pad_cols(x, width):
    return jnp.pad(x, [(0, 0)] * (x.ndim - 1) + [(0, width - x.shape[-1])])


def _pad_rows(x, height, top=0):
    return jnp.pad(x, [(0, 0)] * (x.ndim - 2) + [(top, height - top - x.shape[-2]), (0, 0)])


def _rep(x, n):
    return jnp.repeat(x, n, axis=-1)


def kernel(x, c, ctx, c_ctx, norm_g, w_mod, b_mod, ffn_in, ffn_out, w_in, w_out, gla_w_dec, gla_b_dec, gla_norm, ssm_conv_w, ssm_conv_b, ssm_dt_bias, ssm_a_log, ssm_d, ssm_norm, rwkv_shift_mu, rwkv_w0, rwkv_w_dec, rwkv_a0, rwkv_w_a, rwkv_w_g, rwkv_k_k, rwkv_k_a, rwkv_r_k, rwkv_ln_g, rwkv_ln_b, att_q_norm, att_k_norm):
    bsz, n_lat, d = x.shape
    n_ctx = ctx.shape[1]
    depth = w_mod.shape[0]
    assert bsz == 4 and n_ctx % BLK == 0 and n_lat % BLK == 0 and n_lat % GRID_W == 0
    nbc = n_ctx // BLK
    tri, stri, bd16, eye, bd64, expand = _masks()
    rope_c, rope_sa, rope_sb = _rope_tables(n_ctx, n_lat)

    cvec = jnp.concatenate([c, c_ctx[None], jnp.zeros((8 - bsz - 1, d), F32)], axis=0)
    mod_all = _compute_mod(cvec, w_mod, b_mod).reshape(depth, 8, N_MOD, d)

    w1 = ffn_in.astype(BF16)
    w2 = ffn_out.astype(BF16)
    wo = w_out.astype(BF16)
    o_ssm = 784
    o_rw = o_ssm + 772
    o_att = o_rw + 1024
    w_proj = jnp.concatenate([_pad_cols(w_in[..., :o_ssm], GLA_W), _pad_cols(w_in[..., o_ssm:o_rw], SSM_W),
                              w_in[..., o_rw:o_att], w_in[..., o_att:]], axis=-1).astype(BF16)

    h = x
    for l in range(depth):
        gla_wd = _pad_rows(gla_w_dec[l], LANE)
        gla_bd = gla_b_dec[l][:, None, :]
        ssm_dtb = _rep(ssm_dt_bias[l], SSM_P)[:, None, :]
        ssm_an = _rep(-jnp.exp(ssm_a_log[l]), SSM_P)[:, None, :]
        rw_wd = _pad_rows(rwkv_w_dec[l], LANE)
        rw_wa = _pad_rows(rwkv_w_a[l], LANE, top=64)
        rw_w0 = rwkv_w0[l][:, None, :]

        h = _ffn(h, mod_all, norm_g[l, 0][None], w1, w2, l, 0, nbc, ctx=ctx if l == 0 else None)
        u_gla, u_ssm, u_rw, u_att = _inproj(h, mod_all, norm_g[l, 1][None], w_proj, l, nbc)

        gla_o = _gla_scan(u_gla, gla_wd, gla_bd, tri, nbc)
        ssd_y, ssd_xs = _ssd_scan(u_ssm, ssm_conv_w[l], ssm_conv_b[l][None], ssm_dtb, ssm_an, expand, tri, nbc)
        r, k2, v, g, kk, ab, tlw = _rwkv_prep(u_rw, rwkv_shift_mu[l], rwkv_a0[l][None], rw_wa, rwkv_w_g[l],
                                              rwkv_k_k[l][None], rwkv_k_a[l][None], bd64, nbc)
        rw_y = _rwkv_scan(r, k2, v, kk, ab, tlw, rw_w0, rw_wd, tri, stri, bd16, eye, nbc)
        qn, kt, vt = _att_prep(u_att, rope_c, rope_sa, rope_sb, jnp.tile(att_q_norm[l], ATT_HQ)[None],
                               jnp.tile(att_k_norm[l], ATT_HKV)[None], bd64)
        att_o = _attention(qn, kt, vt, nbc, n_ctx)

        h = _mix_out(h, mod_all, gla_o, u_gla, ssd_y, ssd_xs, u_ssm, rw_y, r, k2, v, g, att_o,
                     jnp.tile(gla_norm[l], GLA_H)[None], _rep(ssm_d[l], SSM_P)[None], ssm_norm[l][None],
                     rwkv_r_k[l].reshape(1, HEAD_W), rwkv_ln_g[l][None], rwkv_ln_b[l][None], bd64,
                     wo, l, nbc)
        h = _ffn(h, mod_all, norm_g[l, 2][None], w1, w2, l, 1, nbc, latent_only=l == depth - 1)
    return h
```

```python
import functools
import math

import jax
import jax.numpy as jnp
from jax import lax
from jax.experimental import pallas as pl
from jax.experimental.pallas import tpu as pltpu

F32 = jnp.float32
BF16 = jnp.bfloat16

N_MOD = 9
EPS = 1e-6
GLA_H, GLA_DK, GLA_DV, GLA_LORA, GLA_TAU = 4, 32, 64, 16, 16.0
SSM_H, SSM_P, SSM_G, SSM_N = 4, 64, 2, 64
RWKV_H, RWKV_N, RWKV_GN_EPS = 4, 64, 64e-5
ATT_HQ, ATT_HKV, ATT_HD = 4, 2, 64
GRID_W = 64
ROPE_THETA = 10000.0
ROPE_AXIS_DIM = ATT_HD // 2

CHUNK = 64
SUB = 16
BLK = 256
NSUB = BLK // CHUNK
ATT_ROWS = 256
LANE = 128
HEAD_W = 256
VMEM_LIMIT = 56 * 1024 * 1024

GLA_W = 896
SSM_W = 896
RWKV_W = 1024
ATT_W = 512


def _cparams(sem):
    return pltpu.CompilerParams(dimension_semantics=sem, vmem_limit_bytes=VMEM_LIMIT)


def _mm(a, b):
    return jnp.dot(a.astype(BF16), b.astype(BF16), preferred_element_type=F32)


def _dg(a, b, dims):
    return lax.dot_general(a, b, (dims, ((), ())), preferred_element_type=F32)


_NN = ((1,), (0,))
_NT = ((1,), (1,))
_TN = ((0,), (0,))


def _split2(x):
    hi = x.astype(BF16)
    lo = (x - hi.astype(F32)).astype(BF16)
    return hi, lo


def _split3(x):
    hi = x.astype(BF16)
    r1 = x - hi.astype(F32)
    mid = r1.astype(BF16)
    lo = (r1 - mid.astype(F32)).astype(BF16)
    return hi, mid, lo


_parts = _split2


def _kdot(a, b):
    ah, al = a
    bh, bl = b
    return _dg(jnp.concatenate([ah, al], axis=1), jnp.concatenate([bh, bl, bh, bl], axis=0), _NN)


def _mm_mask_lhs(mask_bf16, x, dims=_NN):
    hi, mid, lo = _split3(x)
    return _dg(mask_bf16, hi, dims) + (_dg(mask_bf16, mid, dims) + _dg(mask_bf16, lo, dims))


def _mm_mask_rhs(x, mask_bf16, dims=_NN):
    hi, mid, lo = _split3(x)
    return _dg(hi, mask_bf16, dims) + (_dg(mid, mask_bf16, dims) + _dg(lo, mask_bf16, dims))


def _seg_sum(x, mask_bf16):
    hi, lo = _split2(x)
    return _dg(hi, mask_bf16, _NN) + _dg(lo, mask_bf16, _NN)


def _sigmoid(x):
    return jax.nn.sigmoid(x)


def _silu(x):
    return x * jax.nn.sigmoid(x)


def _softplus(x):
    return jnp.maximum(x, 0.0) + jnp.log1p(jnp.exp(-jnp.abs(x)))


def _modulate(x, g, shift, scale):
    ms = jnp.mean(x * x, axis=-1, keepdims=True)
    return (x * lax.rsqrt(ms + EPS) * g) * (1.0 + scale) + shift


def _block_of_step(d, i, nbc, nb):
    back = jnp.where(i < nbc, nbc - 1 - i, nb + nbc - 1 - i)
    return jnp.where(d == 0, i, back)


def _mod_row(b, i, nbc):
    return jnp.where(i < nbc, 4, b)


def _mod_kernel(c_ref, w_ref, b_ref, o_ref):
    o_ref[0] = _mm(_silu(c_ref[...]), w_ref[0]) + b_ref[0]


def _compute_mod(cvec, w_mod, b_mod):
    depth, d, n = w_mod.shape
    tn = 1024
    return pl.pallas_call(
        _mod_kernel,
        grid=(depth, n // tn),
        in_specs=[pl.BlockSpec((8, d), lambda l, j: (0, 0)),
                  pl.BlockSpec((1, d, tn), lambda l, j: (l, 0, j)),
                  pl.BlockSpec((1, 1, tn), lambda l, j: (l, 0, j))],
        out_specs=pl.BlockSpec((1, 8, tn), lambda l, j: (l, 0, j)),
        out_shape=jax.ShapeDtypeStruct((depth, 8, n), F32),
        compiler_params=_cparams(("arbitrary", "arbitrary")),
        name="adaln_mod",
    )(cvec, w_mod, b_mod.reshape(depth, 1, n))


def _ffn_math(x, m, g, w1_ref, w2_ref, i0, d_ff):
    hm = _modulate(x, g, m[i0:i0 + 1], m[i0 + 1:i0 + 2]).astype(BF16)
    ab = jnp.dot(hm, w1_ref[...], preferred_element_type=F32)
    a = ab[:, :d_ff]
    act = (_silu(a) * ab[:, d_ff:]).astype(BF16)
    y = jnp.dot(act, w2_ref[...], preferred_element_type=F32)
    return x + (0.5 * m[i0 + 2:i0 + 3]) * y


def _ffn_kernel(h_ref, mod_ref, g_ref, w1_ref, w2_ref, o_ref, *, i0, d_ff):
    o_ref[0] = _ffn_math(h_ref[0], mod_ref[0], g_ref[...], w1_ref, w2_ref, i0, d_ff)


def _ffn_first_kernel(ctx_ref, x_ref, mod_ref, g_ref, w1_ref, w2_ref, o_ref, *, i0, d_ff, nbc):
    x = jnp.where(pl.program_id(1) < nbc, ctx_ref[0], x_ref[0])
    o_ref[0] = _ffn_math(x, mod_ref[0], g_ref[...], w1_ref, w2_ref, i0, d_ff)


def _ffn_last_kernel(h_ref, mod_ref, g_ref, w1_ref, w2_ref, o_ref, *, i0, d_ff, nbc):
    @pl.when(pl.program_id(1) >= nbc)
    def _():
        o_ref[0] = _ffn_math(h_ref[0], mod_ref[0], g_ref[...], w1_ref, w2_ref, i0, d_ff)


def _mod_spec(layer, nbc, d):
    return pl.BlockSpec((None, 1, N_MOD, d), lambda b, i: (layer, _mod_row(b, i, nbc), 0, 0))


def _ffn(h, mod, g, w1, w2, layer, which, nbc, ctx=None, latent_only=False):
    bsz, _, d = h.shape
    d_ff = w2.shape[2]
    t = h.shape[1] + (0 if ctx is None else ctx.shape[1])
    tok = pl.BlockSpec((1, BLK, d), lambda b, i: (b, i, 0))
    lat = pl.BlockSpec((1, BLK, d), lambda b, i: (b, jnp.maximum(i - nbc, 0), 0))
    common = [_mod_spec(layer, nbc, d),
              pl.BlockSpec((1, d), lambda b, i: (0, 0)),
              pl.BlockSpec((None, None, d, 2 * d_ff), lambda b, i: (layer, which, 0, 0)),
              pl.BlockSpec((None, None, d_ff, d), lambda b, i: (layer, which, 0, 0))]
    if ctx is not None:
        body = functools.partial(_ffn_first_kernel, i0=6 * which, d_ff=d_ff, nbc=nbc)
        ins = [pl.BlockSpec((1, BLK, d), lambda b, i: (b, jnp.minimum(i, nbc - 1), 0)), lat]
        args = (ctx, h)
    else:
        body = functools.partial(_ffn_last_kernel if latent_only else _ffn_kernel, i0=6 * which, d_ff=d_ff,
                                 **({"nbc": nbc} if latent_only else {}))
        ins = [tok]
        args = (h,)
    return pl.pallas_call(
        body,
        grid=(bsz, t // BLK),
        in_specs=ins + common,
        out_specs=lat if latent_only else tok,
        out_shape=jax.ShapeDtypeStruct((bsz, t - nbc * BLK if latent_only else t, d), F32),
        compiler_params=_cparams(("arbitrary", "arbitrary")),
        name="ffn",
    )(*args, mod, g, w1, w2)


def _inproj_kernel(h_ref, mod_ref, g_ref, w_ref, gla_ref, ssm_ref, rwkv_ref, att_ref):
    m = mod_ref[0]
    hm = _modulate(h_ref[0], g_ref[...], m[3:4], m[4:5]).astype(BF16)
    u = jnp.dot(hm, w_ref[...], preferred_element_type=F32)
    gla_ref[0] = u[:, :GLA_W]
    ssm_ref[0] = u[:, GLA_W:GLA_W + SSM_W]
    rwkv_ref[0] = u[:, GLA_W + SSM_W:GLA_W + SSM_W + RWKV_W]
    att_ref[0] = u[:, GLA_W + SSM_W + RWKV_W:]


def _inproj(h, mod, g, w, layer, nbc):
    bsz, t, d = h.shape
    widths = (GLA_W, SSM_W, RWKV_W, ATT_W)
    return pl.pallas_call(
        _inproj_kernel,
        grid=(bsz, t // BLK),
        in_specs=[pl.BlockSpec((1, BLK, d), lambda b, i: (b, i, 0)),
                  _mod_spec(layer, nbc, d),
                  pl.BlockSpec((1, d), lambda b, i: (0, 0)),
                  pl.BlockSpec((None, d, sum(widths)), lambda b, i: (layer, 0, 0))],
        out_specs=[pl.BlockSpec((1, BLK, w_), lambda b, i: (b, i, 0)) for w_ in widths],
        out_shape=[jax.ShapeDtypeStruct((bsz, t, w_), F32) for w_ in widths],
        compiler_params=_cparams(("arbitrary", "arbitrary")),
        name="inproj",
    )(h, mod, g, w)


def _gla_stages(u_ref, wdec_ref, bdec_ref, tri_ref, o_ref, s_ref):
    d = pl.program_id(1)
    i = pl.program_id(2)

    @pl.when(i == 0)
    def _():
        s_ref[...] = jnp.zeros_like(s_ref)

    tri = tri_ref[d]
    tri_b = tri.astype(BF16)
    wdec = wdec_ref[0].astype(BF16)
    bdec = bdec_ref[0]

    rows_of, qh, kih, keh, vh, dec = [], [], [], [], [], []
    for j in range(NSUB):
        jj = jnp.where(d == 0, j, NSUB - 1 - j)
        rows = pl.ds(pl.multiple_of(jj * CHUNK, CHUNK), CHUNK)
        rows_of.append(rows)
        u = u_ref[0, rows, :]
        q = u[:, 0:128] * (GLA_DK ** -0.5)
        k = u[:, 128:256]
        v = u[:, 256:512]
        z = _mm(u[:, 768:896], wdec) + bdec
        la = -_softplus(-z) * (1.0 / GLA_TAU)
        bc = _mm_mask_lhs(tri_b, la)
        bt = jnp.sum(la, axis=0, keepdims=True)
        qd = q * jnp.exp(bc)
        ki = k * jnp.exp(-bc)
        ke = k * jnp.exp(bt - bc)
        dec_j = jnp.exp(bt)
        for h in range(GLA_H):
            ks = slice(GLA_DK * h, GLA_DK * (h + 1))
            qh.append(qd[:, ks].astype(BF16))
            kih.append(ki[:, ks].astype(BF16))
            keh.append(ke[:, ks].astype(BF16))
            vh.append(v[:, GLA_DV * h:GLA_DV * (h + 1)].astype(BF16))
            dec.append(dec_j[:, ks])
        yield
    att = [_dg(a, b, _NT) * tri for a, b in zip(qh, kih)]
    yield
    o_intra = [_mm(a, b) for a, b in zip(att, vh)]
    kv = [_dg(a, b, _TN) for a, b in zip(vh, keh)]
    yield
    st = [s_ref[h] for h in range(GLA_H)]
    s_prev = []
    for j in range(NSUB):
        for h in range(GLA_H):
            n = j * GLA_H + h
            s_prev.append(st[h].astype(BF16))
            st[h] = st[h] * dec[n] + kv[n]
    for h in range(GLA_H):
        s_ref[h] = st[h]
    for j in range(NSUB):
        for h in range(GLA_H):
            n = j * GLA_H + h
            o_ref[0, 0, rows_of[j], GLA_DV * h:GLA_DV * (h + 1)] = o_intra[n] + _dg(qh[n], s_prev[n], _NT)


def _shifted(x, prev_row, next_row):
    n = x.shape[0]
    row = lax.broadcasted_iota(jnp.int32, x.shape, 0)
    xm1 = jnp.where(row == 0, prev_row, pltpu.roll(x, 1, 0))
    xp1 = jnp.where(row == n - 1, next_row, pltpu.roll(x, n - 1, 0))
    return xm1, xp1


def _halo_specs(width, blk_fn):
    per = BLK // 8

    def prev_map(*idx):
        b, blk = blk_fn(*idx)
        return (b, jnp.maximum(blk * per - 1, 0), 0)

    def next_map(nrow8):
        def f(*idx):
            b, blk = blk_fn(*idx)
            return (b, jnp.minimum((blk + 1) * per, nrow8 - 1), 0)
        return f

    return (lambda: pl.BlockSpec((1, 8, width), prev_map),
            lambda nrow8: pl.BlockSpec((1, 8, width), next_map(nrow8)))


def _stream_edges(blk, nbc, nb):
    has_prev = jnp.logical_and(blk != 0, blk != nbc)
    has_next = jnp.logical_and(blk != nbc - 1, blk != nb - 1)
    return has_prev.astype(F32), has_next.astype(F32)


def _ssd_stages(u_ref, up_ref, un_ref, cw_ref, cb_ref, dtb_ref, aneg_ref, exp_ref, tri_ref,
                y_ref, xs_ref, h_ref, xdt_s, bm_s, cm_s, da_s, *, nbc, nb):
    d = pl.program_id(1)
    i = pl.program_id(2)
    blk = _block_of_step(d, i, nbc, nb)

    @pl.when(i == 0)
    def _():
        h_ref[...] = jnp.zeros_like(h_ref)

    has_prev, has_next = _stream_edges(blk, nbc, nb)
    u = u_ref[0]
    xbc = u[:, 256:768]
    prev_row = up_ref[0, 7:8, 256:768] * has_prev
    next_row = un_ref[0, 0:1, 256:768] * has_next
    xm1, xp1 = _shifted(xbc, prev_row, next_row)
    cw = cw_ref[...]
    act = _silu(cw[0:1] * xm1 + cw[1:2] * xbc + cw[2:3] * xp1 + cb_ref[...])
    xs = act[:, 0:256]
    xs_ref[0, 0] = xs
    dt = _softplus(_mm_mask_rhs(u[:, 768:896], exp_ref[...].astype(BF16)) + dtb_ref[0])
    xdt_s[...] = xs * dt
    da_s[...] = dt * aneg_ref[0]
    bm_s[...] = act[:, 256:384]
    cm_s[...] = act[:, 384:512]
    yield

    tri = tri_ref[d]
    tri_b = tri.astype(BF16)

    per_g = SSM_H // SSM_G
    per_l = LANE // SSM_P
    rows_of, cgs, bgs, bgt, xdh, xh, cum_col, cum_row, dec = [], [], [], [], [], [], [], [], []
    for j in range(NSUB):
        jj = jnp.where(d == 0, j, NSUB - 1 - j)
        rows = pl.ds(pl.multiple_of(jj * CHUNK, CHUNK), CHUNK)
        rows_of.append(rows)
        da = da_s[rows, :]
        xdt = xdt_s[rows, :]
        bm = bm_s[rows, :]
        cm = cm_s[rows, :]
        cum = _mm_mask_lhs(tri_b, da)
        tot = jnp.sum(da, axis=0, keepdims=True)
        cum_t = [cum[:, LANE * c:LANE * (c + 1)].T for c in range(HEAD_W // LANE)]
        bm_t = bm.T.astype(BF16)
        for h in range(SSM_H):
            hs = slice(SSM_P * h, SSM_P * (h + 1))
            gs = slice(SSM_N * (h // per_g), SSM_N * (h // per_g + 1))
            cgs.append(cm[:, gs])
            bgs.append(bm[:, gs])
            bgt.append(bm_t[gs, :])
            xdh.append(xdt[:, hs])
            cum_row.append(cum_t[h // per_l][SSM_P * (h % per_l):SSM_P * (h % per_l) + 1, :])
            xh.append(xdt[:, hs].astype(BF16))
            cum_col.append(cum[:, hs])
            dec.append(tot[:, hs])
        yield
    cb = [_dg(c.astype(BF16), b.astype(BF16), _NT) for c, b in zip(cgs[::per_g], bgs[::per_g])]
    seg = [jnp.where(tri > 0, jnp.exp(c - r), 0.0) for c, r in zip(cum_col, cum_row)]
    yield
    y_intra = [_mm(cb[n // per_g] * s, x) for n, (s, x) in enumerate(zip(seg, xh))]
    contrib = [_dg(b, (x * jnp.exp(t - c)).astype(BF16), _NN) for b, t, c, x in zip(bgt, dec, cum_col, xdh)]
    yield
    hp = [h_ref[h] for h in range(SSM_H)]
    h_prev = []
    for j in range(NSUB):
        for h in range(SSM_H):
            n = j * SSM_H + h
            h_prev.append(hp[h])
            hp[h] = hp[h] * jnp.exp(dec[n]) + contrib[n]
    for h in range(SSM_H):
        h_ref[h] = hp[h]
    for j in range(NSUB):
        for h in range(SSM_H):
            n = j * SSM_H + h
            y_ref[0, 0, rows_of[j], SSM_P * h:SSM_P * (h + 1)] = (
                y_intra[n] + _mm(cgs[n] * jnp.exp(cum_col[n]), h_prev[n]))


def _rwkv_prep_kernel(u_ref, up_ref, un_ref, mu_ref, a0_ref, wa_ref, wg_ref, kk_ref, ka_ref, bd_ref,
                      r_ref, k_ref, v_ref, g_ref, kkn_ref, ab_ref, tlw_ref, *, nbc, nb):
    blk = pl.program_id(1)
    has_prev, has_next = _stream_edges(blk, nbc, nb)
    u = u_ref[0]
    xm1, xp1 = _shifted(u, up_ref[0, 7:8, :] * has_prev, un_ref[0, 0:1, :] * has_next)
    mu = mu_ref[...]
    u = u + mu[0:1] * (xm1 - u) + mu[1:2] * (xp1 - u)
    r = u[:, 0:256]
    k = u[:, 256:512]
    lwa = u[:, 768:896]
    a = _sigmoid(a0_ref[...] + _mm(lwa, wa_ref[...]))
    g = _mm(_sigmoid(u[:, 896:1024]), wg_ref[...])
    kk = k * kk_ref[...]
    ss = _seg_sum(kk * kk, bd_ref[...].astype(BF16))
    kk = kk / jnp.maximum(jnp.sqrt(ss), 1e-12)
    r_ref[0] = r
    k_ref[0] = k * (1.0 + (a - 1.0) * ka_ref[...])
    v_ref[0] = u[:, 512:768]
    g_ref[0] = g
    kkn_ref[0] = kk
    ab_ref[0] = kk * a
    tlw_ref[0] = jnp.tanh(lwa)


def _rwkv_prep(u, mu, a0, wa, wg, k_k, k_a, bd64, nbc):
    bsz, t, _ = u.shape
    nb = t // BLK
    prev_spec, next_spec = _halo_specs(RWKV_W, lambda b, i: (b, i))
    full = lambda shape: pl.BlockSpec(shape, lambda b, i: (0,) * len(shape))
    tok = lambda w_: pl.BlockSpec((1, BLK, w_), lambda b, i: (b, i, 0))
    return pl.pallas_call(
        functools.partial(_rwkv_prep_kernel, nbc=nbc, nb=nb),
        grid=(bsz, nb),
        in_specs=[tok(RWKV_W), prev_spec(), next_spec(t // 8),
                  full((2, RWKV_W)), full((1, HEAD_W)), full((LANE, HEAD_W)), full((LANE, HEAD_W)),
                  full((1, HEAD_W)), full((1, HEAD_W)), full((HEAD_W, HEAD_W))],
        out_specs=[tok(HEAD_W)] * 6 + [tok(LANE)],
        out_shape=[jax.ShapeDtypeStruct((bsz, t, HEAD_W), F32)] * 6
        + [jax.ShapeDtypeStruct((bsz, t, LANE), F32)],
        compiler_params=_cparams(("arbitrary", "arbitrary")),
        name="rwkv_prep",
    )(u, u, u, mu, a0, wa, wg, k_k, k_a, bd64)


def _rwkv_stages(r_ref, k_ref, v_ref, kk_ref, ab_ref, tlw_ref, w0_ref, wdec_ref, tri_ref, stri_ref,
                 bd16_ref, eye_ref, y_ref, h_ref):
    d = pl.program_id(1)
    i = pl.program_id(2)

    @pl.when(i == 0)
    def _():
        h_ref[...] = jnp.zeros_like(h_ref)

    incl2 = tri_ref[d]
    strict2 = stri_ref[d]
    incl = incl2[:, :CHUNK]
    strict = strict2[:, :CHUNK]
    incl_b = incl.astype(BF16)
    bd16 = bd16_ref[...]
    eye = eye_ref[...]
    w0 = w0_ref[0]
    wdec = wdec_ref[0].astype(BF16)

    mask2 = jnp.concatenate([strict, incl], axis=0)
    stack = lambda a, b: jnp.concatenate([a, b], axis=0)
    heads = [slice(RWKV_N * h, RWKV_N * (h + 1)) for h in range(RWKV_H)]
    bf = lambda x: x.astype(BF16)

    rows_of, ar, rt, bt, kt, vh, bh, kh, pc = [], [], [], [], [], [], [], [], []
    for j in range(NSUB):
        jj = jnp.where(d == 0, j, NSUB - 1 - j)
        rows = pl.ds(pl.multiple_of(jj * CHUNK, CHUNK), CHUNK)
        rows_of.append(rows)
        wr = w0 + _mm(tlw_ref[0, rows, :], wdec)
        lw = -jnp.exp(-_softplus(-wr) - 0.5)
        gc = _mm_mask_lhs(incl_b, lw)
        tot = jnp.sum(lw, axis=0, keepdims=True)
        eng = jnp.exp(-gc)
        e_end = jnp.exp(tot - gc)
        abv = ab_ref[0, rows, :]
        k2 = k_ref[0, rows, :]
        at_j = bf(-kk_ref[0, rows, :] * jnp.exp(gc - lw))
        rt_f = r_ref[0, rows, :] * jnp.exp(gc)
        rt_j = bf(rt_f)
        bt_j = bf(abv * eng)
        kt_j = bf(k2 * eng)
        v_j = bf(v_ref[0, rows, :])
        bh_j = bf(abv * e_end)
        kh_j = bf(k2 * e_end)
        pc_j = jnp.exp(tot)
        for hs in heads:
            ar.append(stack(at_j[:, hs], rt_j[:, hs]))
            rt.append(rt_f[:, hs])
            bt.append(stack(bt_j[:, hs], bt_j[:, hs]))
            kt.append(kt_j[:, hs])
            vh.append(v_j[:, hs])
            bh.append(bh_j[:, hs])
            kh.append(kh_j[:, hs])
            pc.append(pc_j[:, hs])
        yield

    g1 = [_dg(a, b, _NT) for a, b in zip(ar, bt)]
    g2 = [_dg(a, b, _NT) for a, b in zip(ar, kt)]
    yield
    a_ab = [g[:CHUNK] * strict2 for g in g1]
    a_rb = [bf(g[CHUNK:, :CHUNK] * incl) for g in g1]
    avr = [_dg(bf(g * mask2), v, _NN) for g, v in zip(g2, vh)]
    khv = [_dg(v, k, _TN) for v, k in zip(vh, kh)]
    yield
    ad = [a * bd16 for a in a_ab]
    ee = [_parts(a - b) for a, b in zip(a_ab, ad)]
    q = [eye + a for a in ad]
    pw = [_kdot(x, x) for x in map(_parts, ad)]
    yield
    for _ in range(2):
        res = [_kdot(_parts(stack(a, x)), _parts(a)) for a, x in zip(pw, q)]
        pw = [r[:CHUNK] for r in res]
        q = [x + r[CHUNK:] for x, r in zip(q, res)]
        yield
    p = [x + _kdot(_parts(x), _parts(a)) for x, a in zip(q, pw)]
    yield
    p_s = [_parts(x) for x in p]
    f = [_parts(_kdot(x, e)) for x, e in zip(p_s, ee)]
    yield
    f2 = [_parts(_kdot(x, x)) for x in f]
    t1 = [x + _kdot(a, xs) for x, a, xs in zip(p, f, p_s)]
    yield
    tinv = [bf((x + _kdot(a, _parts(x)))[:, :CHUNK]) for x, a in zip(t1, f2)]
    z = [jnp.concatenate([a[:CHUNK], bf(c[:CHUNK])], axis=1) for a, c in zip(ar, avr)]
    yield
    tz = [bf(_dg(t, x, _NN)) for t, x in zip(tinv, z)]
    w = [bf(_dg(t, b, _TN)) for t, b in zip(tinv, bh)]
    yield
    rz = [_dg(b, x, _NN) for b, x in zip(a_rb, tz)]
    mn = [_dg(x, y, _TN) for x, y in zip(z, w)]
    qm = [bf(r + x[:, :CHUNK]) for r, x in zip(rt, rz)]
    y0 = [c[CHUNK:] + x[:, CHUNK:] for c, x in zip(avr, rz)]
    m1 = [bf(x[:CHUNK]) for x in mn]
    n1 = [x[CHUNK:] + e for x, e in zip(mn, khv)]
    yield

    ht = [h_ref[h] for h in range(RWKV_H)]
    for j in range(NSUB):
        for h in range(RWKV_H):
            n = j * RWKV_H + h
            hb = bf(ht[h])
            y_ref[0, 0, rows_of[j], heads[h]] = _dg(qm[n], hb, _NT) + y0[n]
            ht[h] = ht[h] * pc[n] + _dg(hb, m1[n], _NN) + n1[n]
        yield
    for h in range(RWKV_H):
        h_ref[h] = ht[h]


def _interleave(main, side):
    k = 0
    for _ in main:
        for _ in range(len(side)):
            gen = side[k % len(side)]
            k += 1
            if next(gen, StopIteration) is not StopIteration:
                break
    for gen in side:
        for _ in gen:
            pass


def _scan_kernel(ug_ref, gwd_ref, gbd_ref, us_ref, up_ref, un_ref, cw_ref, cb_ref, dtb_ref, aneg_ref, exp_ref,
                 r_ref, k_ref, v_ref, kk_ref, ab_ref, tlw_ref, w0_ref, wdec_ref,
                 tri_ref, tri2_ref, stri2_ref, bd16_ref, eye_ref,
                 gla_o_ref, ssd_y_ref, ssd_xs_ref, rw_y_ref,
                 gla_s, ssd_h, xdt_s, bm_s, cm_s, da_s, rw_h, *, nbc, nb):
    rwkv = _rwkv_stages(r_ref, k_ref, v_ref, kk_ref, ab_ref, tlw_ref, w0_ref, wdec_ref, tri2_ref, stri2_ref,
                        bd16_ref, eye_ref, rw_y_ref, rw_h)
    ssd = _ssd_stages(us_ref, up_ref, un_ref, cw_ref, cb_ref, dtb_ref, aneg_ref, exp_ref, tri_ref,
                      ssd_y_ref, ssd_xs_ref, ssd_h, xdt_s, bm_s, cm_s, da_s, nbc=nbc, nb=nb)
    gla = _gla_stages(ug_ref, gwd_ref, gbd_ref, tri_ref, gla_o_ref, gla_s)
    _interleave(rwkv, [ssd, gla])


def _scans(u_gla, gla_wd, gla_bd, u_ssm, conv_w, conv_b, dtb, aneg, expand,
           r, k2, v, kk, ab, tlw, rw_w0, rw_wd, tri, stri, bd16, eye, nbc):
    bsz, t, _ = r.shape
    nb = t // BLK
    blk = lambda b, d, i: _block_of_step(d, i, nbc, nb)
    tok = lambda w_: pl.BlockSpec((1, BLK, w_), lambda b, d, i: (b, blk(b, d, i), 0))
    full = lambda shape: pl.BlockSpec(shape, lambda b, d, i: (0,) * len(shape))
    by_dir = lambda *shape: pl.BlockSpec((1,) + shape, lambda b, d, i: (d,) + (0,) * len(shape))
    prev_spec, next_spec = _halo_specs(SSM_W, lambda b, d, i: (b, blk(b, d, i)))
    dup = lambda m: jnp.concatenate([m, m], axis=-1)
    out = pl.BlockSpec((1, 1, BLK, HEAD_W), lambda b, d, i: (d, b, blk(b, d, i), 0))
    return pl.pallas_call(
        functools.partial(_scan_kernel, nbc=nbc, nb=nb),
        grid=(bsz, 2, nb),
        in_specs=[tok(GLA_W), by_dir(LANE, LANE), by_dir(1, LANE),
                  tok(SSM_W), prev_spec(), next_spec(t // 8), full((3, 512)), full((1, 512)),
                  by_dir(1, HEAD_W), by_dir(1, HEAD_W), full((LANE, HEAD_W))]
        + [tok(HEAD_W)] * 5 + [tok(LANE), by_dir(1, HEAD_W), by_dir(LANE, HEAD_W),
                               full((2, CHUNK, CHUNK)), full((2, CHUNK, LANE)), full((2, CHUNK, LANE)),
                               full((CHUNK, LANE)), full((CHUNK, LANE))],
        out_specs=[out] * 4,
        out_shape=[jax.ShapeDtypeStruct((2, bsz, t, HEAD_W), F32)] * 4,
        scratch_shapes=[pltpu.VMEM((GLA_H, GLA_DV, GLA_DK), F32), pltpu.VMEM((SSM_H, SSM_N, SSM_P), F32),
                        pltpu.VMEM((BLK, HEAD_W), F32), pltpu.VMEM((BLK, LANE), F32),
                        pltpu.VMEM((BLK, LANE), F32), pltpu.VMEM((BLK, HEAD_W), F32),
                        pltpu.VMEM((RWKV_H, RWKV_N, RWKV_N), F32)],
        compiler_params=_cparams(("arbitrary", "arbitrary", "arbitrary")),
        name="scans",
    )(u_gla, gla_wd, gla_bd, u_ssm, u_ssm, u_ssm, conv_w, conv_b, dtb, aneg, expand,
      r, k2, v, kk, ab, tlw, rw_w0, rw_wd, tri, dup(tri), dup(stri), dup(bd16), dup(eye))


def _rope(x, c, sa, sb):
    w = x.shape[1]
    return x * c + pltpu.roll(x, w - ROPE_AXIS_DIM // 2, 1) * sa + pltpu.roll(x, ROPE_AXIS_DIM // 2, 1) * sb


def _att_prep_kernel(u_ref, c_ref, sa_ref, sb_ref, qg_ref, kg_ref, bd_ref, q_ref, kt_ref, v_ref):
    u = u_ref[0]
    bd = bd_ref[...].astype(BF16)
    c, sa, sb = c_ref[...], sa_ref[...], sb_ref[...]
    c2 = jnp.concatenate([c, c], axis=1)
    sa2 = jnp.concatenate([sa, sa], axis=1)
    sb2 = jnp.concatenate([sb, sb], axis=1)
    q = u[:, 0:256]
    k = u[:, 256:384]
    qms = _seg_sum(q * q, bd) * (1.0 / ATT_HD)
    kms = _seg_sum(k * k, bd[:LANE, :LANE]) * (1.0 / ATT_HD)
    qn = q * lax.rsqrt(qms + EPS) * qg_ref[...]
    kn = k * lax.rsqrt(kms + EPS) * kg_ref[...]
    q_ref[0] = _rope(qn, c2, sa2, sb2) * (ATT_HD ** -0.5)
    knt = _rope(kn, c, sa, sb).T
    kt_ref[0, 0] = knt[:ATT_HD].astype(BF16)
    kt_ref[0, 1] = knt[ATT_HD:].astype(BF16)
    v_ref[0, 0] = u[:, 384:448].astype(BF16)
    v_ref[0, 1] = u[:, 448:512].astype(BF16)


def _att_prep(u, c, sa, sb, qg, kg, bd64):
    bsz, t, _ = u.shape
    nb = t // BLK
    full = lambda shape: pl.BlockSpec(shape, lambda b, i: (0,) * len(shape))
    tab = pl.BlockSpec((BLK, LANE), lambda b, i: (i, 0))
    return pl.pallas_call(
        _att_prep_kernel,
        grid=(bsz, nb),
        in_specs=[pl.BlockSpec((1, BLK, ATT_W), lambda b, i: (b, i, 0)), tab, tab, tab,
                  full((1, HEAD_W)), full((1, LANE)), full((HEAD_W, HEAD_W))],
        out_specs=[pl.BlockSpec((1, BLK, HEAD_W), lambda b, i: (b, i, 0)),
                   pl.BlockSpec((1, ATT_HKV, ATT_HD, BLK), lambda b, i: (b, 0, 0, i)),
                   pl.BlockSpec((1, ATT_HKV, BLK, ATT_HD), lambda b, i: (b, 0, i, 0))],
        out_shape=[jax.ShapeDtypeStruct((bsz, t, HEAD_W), F32),
                   jax.ShapeDtypeStruct((bsz, ATT_HKV, ATT_HD, t), BF16),
                   jax.ShapeDtypeStruct((bsz, ATT_HKV, t, ATT_HD), BF16)],
        compiler_params=_cparams(("arbitrary", "arbitrary")),
        name="att_prep",
    )(u, c, sa, sb, qg, kg, bd64)


def _att_kernel(q_ref, kt_ref, v_ref, o_ref, *, nbc, n_ctx):
    i = pl.program_id(2)

    def attend(n_keys):
        q = q_ref[0]
        for h in range(ATT_HQ // ATT_HKV):
            for r in range(BLK // ATT_ROWS):
                rows = slice(r * ATT_ROWS, (r + 1) * ATT_ROWS)
                qs = q[rows, h * ATT_HD:(h + 1) * ATT_HD].astype(BF16)
                s = jnp.dot(qs, kt_ref[0, 0, :, :n_keys], preferred_element_type=F32)
                e = jnp.exp(s - jnp.max(s, axis=-1, keepdims=True))
                l = jnp.sum(e, axis=-1, keepdims=True)
                o = jnp.dot(e.astype(BF16), v_ref[0, 0, :n_keys, :], preferred_element_type=F32) / l
                o_ref[0, rows, h * ATT_HD:(h + 1) * ATT_HD] = o

    @pl.when(i < nbc)
    def _():
        attend(n_ctx)

    @pl.when(i >= nbc)
    def _():
        attend(kt_ref.shape[3])


def _attention(q, kt, v, nbc, n_ctx):
    bsz, t, _ = q.shape
    nb = t // BLK
    return pl.pallas_call(
        functools.partial(_att_kernel, nbc=nbc, n_ctx=n_ctx),
        grid=(bsz, ATT_HKV, nb),
        in_specs=[pl.BlockSpec((1, BLK, LANE), lambda b, g, i: (b, i, g)),
                  pl.BlockSpec((1, 1, ATT_HD, t), lambda b, g, i: (b, g, 0, 0)),
                  pl.BlockSpec((1, 1, t, ATT_HD), lambda b, g, i: (b, g, 0, 0))],
        out_specs=pl.BlockSpec((1, BLK, LANE), lambda b, g, i: (b, i, g)),
        out_shape=jax.ShapeDtypeStruct((bsz, t, HEAD_W), F32),
        compiler_params=_cparams(("arbitrary", "arbitrary", "arbitrary")),
        name="attention",
    )(q, kt, v)


def _out_kernel(h_ref, mod_ref, glaf_ref, glab_ref, glag_ref, ssdf_ref, ssdb_ref, xs_ref, z_ref,
                rwf_ref, rwb_ref, r_ref, k_ref, v_ref, g_ref, att_ref,
                glan_ref, ssd_d_ref, ssdn_ref, rk_ref, lng_ref, lnb_ref, bd_ref, w_ref, o_ref):
    bd = bd_ref[...].astype(BF16)
    seg_mean = lambda x: _seg_sum(x, bd) * (1.0 / 64.0)
    o = glaf_ref[0, 0] + glab_ref[0, 0]
    y_gla = o * lax.rsqrt(seg_mean(o * o) + EPS) * glan_ref[...] * _silu(glag_ref[0])
    y = ssdf_ref[0, 0] + ssdb_ref[0, 0] + ssd_d_ref[...] * xs_ref[0, 0]
    y = y * _silu(z_ref[0])
    y_ssd = y * lax.rsqrt(jnp.mean(y * y, axis=-1, keepdims=True) + EPS) * ssdn_ref[...]
    y = rwf_ref[0, 0] + rwb_ref[0, 0]
    mu = seg_mean(y)
    yc = y - mu
    var = seg_mean(yc * yc)
    yn = yc * lax.rsqrt(var + RWKV_GN_EPS) * lng_ref[...] + lnb_ref[...]
    v = v_ref[0]
    bonus = _seg_sum(r_ref[0] * k_ref[0] * rk_ref[...], bd) * v
    y_rw = (yn + bonus) * g_ref[0]
    w = w_ref[...]
    proj = (_mm(y_gla, w[0:256]) + _mm(y_ssd, w[256:512])) + (_mm(y_rw, w[512:768]) + _mm(att_ref[0], w[768:1024]))
    o_ref[0] = h_ref[0] + mod_ref[0][5:6] * proj


def _mix_out(h, mod, gla_o, u_gla, ssd_y, ssd_xs, u_ssm, rw_y, r, k2, v, g, att_o,
             gla_n, ssd_d, ssd_n, r_k, ln_g, ln_b, bd64, w_out, layer, nbc):
    bsz, t, d = h.shape
    tok = lambda w_, col=0: pl.BlockSpec((1, BLK, w_), lambda b, i: (b, i, col))
    dirn = lambda dd: pl.BlockSpec((1, 1, BLK, HEAD_W), lambda b, i: (dd, b, i, 0))
    full = lambda shape: pl.BlockSpec(shape, lambda b, i: (0,) * len(shape))
    vec = full((1, HEAD_W))
    return pl.pallas_call(
        _out_kernel,
        grid=(bsz, t // BLK),
        in_specs=[tok(d), _mod_spec(layer, nbc, d),
                  dirn(0), dirn(1), tok(HEAD_W, 2),
                  dirn(0), dirn(1), dirn(0), tok(HEAD_W, 0),
                  dirn(0), dirn(1), tok(HEAD_W), tok(HEAD_W), tok(HEAD_W), tok(HEAD_W), tok(HEAD_W),
                  vec, vec, vec, vec, vec, vec, full((HEAD_W, HEAD_W)),
                  pl.BlockSpec((None, d, d), lambda b, i: (layer, 0, 0))],
        out_specs=tok(d),
        out_shape=jax.ShapeDtypeStruct(h.shape, F32),
        compiler_params=_cparams(("arbitrary", "arbitrary")),
        name="mix_out",
    )(h, mod, gla_o, gla_o, u_gla, ssd_y, ssd_y, ssd_xs, u_ssm, rw_y, rw_y, r, k2, v, g, att_o,
      gla_n, ssd_d, ssd_n, r_k, ln_g, ln_b, bd64, w_out)


def _rope_tables(n_ctx, n_lat):
    rows = n_lat // GRID_W
    row = jnp.repeat(jnp.arange(rows, dtype=F32), GRID_W)
    col = jnp.tile(jnp.arange(GRID_W, dtype=F32), rows)
    inv = ROPE_THETA ** (-jnp.arange(0, ROPE_AXIS_DIM, 2, dtype=F32) / ROPE_AXIS_DIM)
    ang = jnp.stack([row[:, None] * inv, col[:, None] * inv], axis=1)
    cos, sin = jnp.cos(ang), jnp.sin(ang)
    zero = jnp.zeros_like(sin)
    c = jnp.concatenate([cos, cos], axis=-1).reshape(n_lat, ATT_HD)
    sa = jnp.concatenate([-sin, zero], axis=-1).reshape(n_lat, ATT_HD)
    sb = jnp.concatenate([zero, sin], axis=-1).reshape(n_lat, ATT_HD)
    pad = lambda x, fill: jnp.concatenate([jnp.full((n_ctx, ATT_HD), fill, F32), x], axis=0)
    two = lambda x: jnp.concatenate([x, x], axis=1)
    return two(pad(c, 1.0)), two(pad(sa, 0.0)), two(pad(sb, 0.0))


def _masks():
    t = jnp.arange(CHUNK)
    lower = (t[None, :] <= t[:, None]).astype(F32)
    slower = (t[None, :] < t[:, None]).astype(F32)
    tri = jnp.stack([lower, lower.T])
    stri = jnp.stack([slower, slower.T])
    bd16 = (t[None, :] // SUB == t[:, None] // SUB).astype(F32)
    eye = jnp.eye(CHUNK, dtype=F32)
    c = jnp.arange(HEAD_W)
    bd64 = (c[None, :] // 64 == c[:, None] // 64).astype(F32)
    expand = (jnp.arange(LANE)[:, None] == c[None, :] // 64).astype(F32)
    return tri, stri, bd16, eye, bd64, expand


def _pad_cols(x, width):
    return jnp.pad(x, [(0, 0)] * (x.ndim - 1) + [(0, width - x.shape[-1])])


def _pad_rows(x, height, top=0):
    return jnp.pad(x, [(0, 0)] * (x.ndim - 2) + [(top, height - top - x.shape[-2]), (0, 0)])


def _rep(x, n):
    return jnp.repeat(x, n, axis=-1)


def kernel(x, c, ctx, c_ctx, norm_g, w_mod, b_mod, ffn_in, ffn_out, w_in, w_out, gla_w_dec, gla_b_dec, gla_norm, ssm_conv_w, ssm_conv_b, ssm_dt_bias, ssm_a_log, ssm_d, ssm_norm, rwkv_shift_mu, rwkv_w0, rwkv_w_dec, rwkv_a0, rwkv_w_a, rwkv_w_g, rwkv_k_k, rwkv_k_a, rwkv_r_k, rwkv_ln_g, rwkv_ln_b, att_q_norm, att_k_norm):
    bsz, n_lat, d = x.shape
    n_ctx = ctx.shape[1]
    depth = w_mod.shape[0]
    assert bsz == 4 and n_ctx % BLK == 0 and n_lat % BLK == 0 and n_lat % GRID_W == 0
    nbc = n_ctx // BLK
    tri, stri, bd16, eye, bd64, expand = _masks()
    rope_c, rope_sa, rope_sb = _rope_tables(n_ctx, n_lat)

    cvec = jnp.concatenate([c, c_ctx[None], jnp.zeros((8 - bsz - 1, d), F32)], axis=0)
    mod_all = _compute_mod(cvec, w_mod, b_mod).reshape(depth, 8, N_MOD, d)

    w1 = ffn_in.astype(BF16)
    w2 = ffn_out.astype(BF16)
    wo = w_out.astype(BF16)
    o_ssm = 784
    o_rw = o_ssm + 772
    o_att = o_rw + 1024
    w_proj = jnp.concatenate([_pad_cols(w_in[..., :o_ssm], GLA_W), _pad_cols(w_in[..., o_ssm:o_rw], SSM_W),
                              w_in[..., o_rw:o_att], w_in[..., o_att:]], axis=-1).astype(BF16)

    h = x
    for l in range(depth):
        gla_wd = _pad_rows(gla_w_dec[l], LANE)
        gla_bd = gla_b_dec[l][:, None, :]
        ssm_dtb = _rep(ssm_dt_bias[l], SSM_P)[:, None, :]
        ssm_an = _rep(-jnp.exp(ssm_a_log[l]), SSM_P)[:, None, :]
        rw_wd = _pad_rows(rwkv_w_dec[l], LANE)
        rw_wa = _pad_rows(rwkv_w_a[l], LANE, top=64)
        rw_w0 = rwkv_w0[l][:, None, :]

        h = _ffn(h, mod_all, norm_g[l, 0][None], w1, w2, l, 0, nbc, ctx=ctx if l == 0 else None)
        u_gla, u_ssm, u_rw, u_att = _inproj(h, mod_all, norm_g[l, 1][None], w_proj, l, nbc)

        r, k2, v, g, kk, ab, tlw = _rwkv_prep(u_rw, rwkv_shift_mu[l], rwkv_a0[l][None], rw_wa, rwkv_w_g[l],
                                              rwkv_k_k[l][None], rwkv_k_a[l][None], bd64, nbc)
        gla_o, ssd_y, ssd_xs, rw_y = _scans(u_gla, gla_wd, gla_bd, u_ssm, ssm_conv_w[l], ssm_conv_b[l][None], ssm_dtb,
                                            ssm_an, expand, r, k2, v, kk, ab, tlw, rw_w0, rw_wd, tri, stri, bd16,
                                            eye, nbc)
        qn, kt, vt = _att_prep(u_att, rope_c, rope_sa, rope_sb, jnp.tile(att_q_norm[l], ATT_HQ)[None],
                               jnp.tile(att_k_norm[l], ATT_HKV)[None], bd64)
        att_o = _attention(qn, kt, vt, nbc, n_ctx)

        h = _mix_out(h, mod_all, gla_o, u_gla, ssd_y, ssd_xs, u_ssm, rw_y, r, k2, v, g, att_o,
                     jnp.tile(gla_norm[l], GLA_H)[None], _rep(ssm_d[l], SSM_P)[None], ssm_norm[l][None],
                     rwkv_r_k[l].reshape(1, HEAD_W), rwkv_ln_g[l][None], rwkv_ln_b[l][None], bd64,
                     wo, l, nbc)
        h = _ffn(h, mod_all, norm_g[l, 2][None], w1, w2, l, 1, nbc, latent_only=l == depth - 1)
    return h
```

```python
import functools
import math

import jax
import jax.numpy as jnp
from jax import lax
from jax.experimental import pallas as pl
from jax.experimental.pallas import tpu as pltpu

F32 = jnp.float32
BF16 = jnp.bfloat16

N_MOD = 9
EPS = 1e-6
GLA_H, GLA_DK, GLA_DV, GLA_LORA, GLA_TAU = 4, 32, 64, 16, 16.0
SSM_H, SSM_P, SSM_G, SSM_N = 4, 64, 2, 64
RWKV_H, RWKV_N, RWKV_GN_EPS = 4, 64, 64e-5
ATT_HQ, ATT_HKV, ATT_HD = 4, 2, 64
GRID_W = 64
ROPE_THETA = 10000.0
ROPE_AXIS_DIM = ATT_HD // 2

CHUNK = 64
SUB = 16
BLK = 256
NSUB = BLK // CHUNK
LANE = 128
HEAD_W = 256
VMEM_LIMIT = 56 * 1024 * 1024

GLA_W = 896
SSM_W = 896
RWKV_W = 1024
ATT_W = 512


def _cparams(sem):
    return pltpu.CompilerParams(dimension_semantics=sem, vmem_limit_bytes=VMEM_LIMIT)


def _mm(a, b):
    return jnp.dot(a.astype(BF16), b.astype(BF16), preferred_element_type=F32)


def _dg(a, b, dims):
    return lax.dot_general(a, b, (dims, ((), ())), preferred_element_type=F32)


_NN = ((1,), (0,))
_NT = ((1,), (1,))
_TN = ((0,), (0,))


def _split2(x):
    hi = x.astype(BF16)
    lo = (x - hi.astype(F32)).astype(BF16)
    return hi, lo


def _split3(x):
    hi = x.astype(BF16)
    r1 = x - hi.astype(F32)
    mid = r1.astype(BF16)
    lo = (r1 - mid.astype(F32)).astype(BF16)
    return hi, mid, lo


_parts = _split2


def _kdot(a, b):
    ah, al = a
    bh, bl = b
    return _dg(jnp.concatenate([ah, al], axis=1), jnp.concatenate([bh, bl, bh, bl], axis=0), _NN)


def _mm_mask_lhs(mask_bf16, x, dims=_NN):
    hi, mid, lo = _split3(x)
    return _dg(mask_bf16, hi, dims) + (_dg(mask_bf16, mid, dims) + _dg(mask_bf16, lo, dims))


def _mm_mask_rhs(x, mask_bf16, dims=_NN):
    hi, mid, lo = _split3(x)
    return _dg(hi, mask_bf16, dims) + (_dg(mid, mask_bf16, dims) + _dg(lo, mask_bf16, dims))


def _seg_sum(x, mask_bf16):
    hi, lo = _split2(x)
    return _dg(hi, mask_bf16, _NN) + _dg(lo, mask_bf16, _NN)


def _sigmoid(x):
    return jax.nn.sigmoid(x)


def _silu(x):
    return x * jax.nn.sigmoid(x)


def _softplus(x):
    return jnp.maximum(x, 0.0) + jnp.log1p(jnp.exp(-jnp.abs(x)))


def _modulate(x, g, shift, scale):
    ms = jnp.mean(x * x, axis=-1, keepdims=True)
    return (x * lax.rsqrt(ms + EPS) * g) * (1.0 + scale) + shift


def _block_of_step(d, i, nbc, nb):
    back = jnp.where(i < nbc, nbc - 1 - i, nb + nbc - 1 - i)
    return jnp.where(d == 0, i, back)


def _mod_row(b, i, nbc):
    return jnp.where(i < nbc, 4, b)


def _mod_kernel(c_ref, w_ref, b_ref, o_ref):
    o_ref[0] = _mm(_silu(c_ref[...]), w_ref[0]) + b_ref[0]


def _compute_mod(cvec, w_mod, b_mod):
    depth, d, n = w_mod.shape
    tn = 1024
    return pl.pallas_call(
        _mod_kernel,
        grid=(depth, n // tn),
        in_specs=[pl.BlockSpec((8, d), lambda l, j: (0, 0)),
                  pl.BlockSpec((1, d, tn), lambda l, j: (l, 0, j)),
                  pl.BlockSpec((1, 1, tn), lambda l, j: (l, 0, j))],
        out_specs=pl.BlockSpec((1, 8, tn), lambda l, j: (l, 0, j)),
        out_shape=jax.ShapeDtypeStruct((depth, 8, n), F32),
        compiler_params=_cparams(("arbitrary", "arbitrary")),
        name="adaln_mod",
    )(cvec, w_mod, b_mod.reshape(depth, 1, n))


def _ffn_math(x, m, g, w1_ref, w2_ref, i0, d_ff):
    hm = _modulate(x, g, m[i0:i0 + 1], m[i0 + 1:i0 + 2]).astype(BF16)
    ab = jnp.dot(hm, w1_ref[...], preferred_element_type=F32)
    a = ab[:, :d_ff]
    act = (_silu(a) * ab[:, d_ff:]).astype(BF16)
    y = jnp.dot(act, w2_ref[...], preferred_element_type=F32)
    return x + (0.5 * m[i0 + 2:i0 + 3]) * y


def _ffn_kernel(h_ref, mod_ref, g_ref, w1_ref, w2_ref, o_ref, *, i0, d_ff):
    o_ref[0] = _ffn_math(h_ref[0], mod_ref[0], g_ref[...], w1_ref, w2_ref, i0, d_ff)


def _ffn_first_kernel(ctx_ref, x_ref, mod_ref, g_ref, w1_ref, w2_ref, o_ref, *, i0, d_ff, nbc):
    x = jnp.where(pl.program_id(1) < nbc, ctx_ref[0], x_ref[0])
    o_ref[0] = _ffn_math(x, mod_ref[0], g_ref[...], w1_ref, w2_ref, i0, d_ff)


def _ffn_last_kernel(h_ref, mod_ref, g_ref, w1_ref, w2_ref, o_ref, *, i0, d_ff, nbc):
    @pl.when(pl.program_id(1) >= nbc)
    def _():
        o_ref[0] = _ffn_math(h_ref[0], mod_ref[0], g_ref[...], w1_ref, w2_ref, i0, d_ff)


def _mod_spec(layer, nbc, d):
    return pl.BlockSpec((None, 1, N_MOD, d), lambda b, i: (layer, _mod_row(b, i, nbc), 0, 0))


def _ffn(h, mod, g, w1, w2, layer, which, nbc, ctx=None, latent_only=False):
    bsz, _, d = h.shape
    d_ff = w2.shape[2]
    t = h.shape[1] + (0 if ctx is None else ctx.shape[1])
    tok = pl.BlockSpec((1, BLK, d), lambda b, i: (b, i, 0))
    lat = pl.BlockSpec((1, BLK, d), lambda b, i: (b, jnp.maximum(i - nbc, 0), 0))
    common = [_mod_spec(layer, nbc, d),
              pl.BlockSpec((1, d), lambda b, i: (0, 0)),
              pl.BlockSpec((None, None, d, 2 * d_ff), lambda b, i: (layer, which, 0, 0)),
              pl.BlockSpec((None, None, d_ff, d), lambda b, i: (layer, which, 0, 0))]
    if ctx is not None:
        body = functools.partial(_ffn_first_kernel, i0=6 * which, d_ff=d_ff, nbc=nbc)
        ins = [pl.BlockSpec((1, BLK, d), lambda b, i: (b, jnp.minimum(i, nbc - 1), 0)), lat]
        args = (ctx, h)
    else:
        body = functools.partial(_ffn_last_kernel if latent_only else _ffn_kernel, i0=6 * which, d_ff=d_ff,
                                 **({"nbc": nbc} if latent_only else {}))
        ins = [tok]
        args = (h,)
    return pl.pallas_call(
        body,
        grid=(bsz, t // BLK),
        in_specs=ins + common,
        out_specs=lat if latent_only else tok,
        out_shape=jax.ShapeDtypeStruct((bsz, t - nbc * BLK if latent_only else t, d), F32),
        compiler_params=_cparams(("arbitrary", "arbitrary")),
        name="ffn",
    )(*args, mod, g, w1, w2)


def _inproj_kernel(h_ref, mod_ref, g_ref, w_ref, gla_ref, ssm_ref, rwkv_ref, att_ref):
    m = mod_ref[0]
    hm = _modulate(h_ref[0], g_ref[...], m[3:4], m[4:5]).astype(BF16)
    u = jnp.dot(hm, w_ref[...], preferred_element_type=F32)
    gla_ref[0] = u[:, :GLA_W]
    ssm_ref[0] = u[:, GLA_W:GLA_W + SSM_W]
    rwkv_ref[0] = u[:, GLA_W + SSM_W:GLA_W + SSM_W + RWKV_W]
    att_ref[0] = u[:, GLA_W + SSM_W + RWKV_W:]


def _inproj(h, mod, g, w, layer, nbc):
    bsz, t, d = h.shape
    widths = (GLA_W, SSM_W, RWKV_W, ATT_W)
    return pl.pallas_call(
        _inproj_kernel,
        grid=(bsz, t // BLK),
        in_specs=[pl.BlockSpec((1, BLK, d), lambda b, i: (b, i, 0)),
                  _mod_spec(layer, nbc, d),
                  pl.BlockSpec((1, d), lambda b, i: (0, 0)),
                  pl.BlockSpec((None, d, sum(widths)), lambda b, i: (layer, 0, 0))],
        out_specs=[pl.BlockSpec((1, BLK, w_), lambda b, i: (b, i, 0)) for w_ in widths],
        out_shape=[jax.ShapeDtypeStruct((bsz, t, w_), F32) for w_ in widths],
        compiler_params=_cparams(("arbitrary", "arbitrary")),
        name="inproj",
    )(h, mod, g, w)


def _gla_stages(u_ref, wdec_ref, bdec_ref, tri_ref, o_ref, s_ref):
    d = pl.program_id(1)
    i = pl.program_id(2)

    @pl.when(i == 0)
    def _():
        s_ref[...] = jnp.zeros_like(s_ref)

    tri = tri_ref[d]
    tri_b = tri.astype(BF16)
    wdec = wdec_ref[0].astype(BF16)
    bdec = bdec_ref[0]

    rows_of, qh, kih, keh, vh, dec = [], [], [], [], [], []
    for j in range(NSUB):
        jj = jnp.where(d == 0, j, NSUB - 1 - j)
        rows = pl.ds(pl.multiple_of(jj * CHUNK, CHUNK), CHUNK)
        rows_of.append(rows)
        u = u_ref[0, rows, :]
        q = u[:, 0:128] * (GLA_DK ** -0.5)
        k = u[:, 128:256]
        v = u[:, 256:512]
        z = _mm(u[:, 768:896], wdec) + bdec
        la = -_softplus(-z) * (1.0 / GLA_TAU)
        bc = _mm_mask_lhs(tri_b, la)
        bt = jnp.sum(la, axis=0, keepdims=True)
        qd = q * jnp.exp(bc)
        ki = k * jnp.exp(-bc)
        ke = k * jnp.exp(bt - bc)
        dec_j = jnp.exp(bt)
        for h in range(GLA_H):
            ks = slice(GLA_DK * h, GLA_DK * (h + 1))
            qh.append(qd[:, ks].astype(BF16))
            kih.append(ki[:, ks].astype(BF16))
            keh.append(ke[:, ks].astype(BF16))
            vh.append(v[:, GLA_DV * h:GLA_DV * (h + 1)].astype(BF16))
            dec.append(dec_j[:, ks])
        yield
    att = [_dg(a, b, _NT) * tri for a, b in zip(qh, kih)]
    yield
    o_intra = [_mm(a, b) for a, b in zip(att, vh)]
    kv = [_dg(a, b, _TN) for a, b in zip(vh, keh)]
    yield
    st = [s_ref[h] for h in range(GLA_H)]
    s_prev = []
    for j in range(NSUB):
        for h in range(GLA_H):
            n = j * GLA_H + h
            s_prev.append(st[h].astype(BF16))
            st[h] = st[h] * dec[n] + kv[n]
    for h in range(GLA_H):
        s_ref[h] = st[h]
    for j in range(NSUB):
        for h in range(GLA_H):
            n = j * GLA_H + h
            o_ref[0, 0, rows_of[j], GLA_DV * h:GLA_DV * (h + 1)] = o_intra[n] + _dg(qh[n], s_prev[n], _NT)


def _shifted(x, prev_row, next_row):
    n = x.shape[0]
    row = lax.broadcasted_iota(jnp.int32, x.shape, 0)
    xm1 = jnp.where(row == 0, prev_row, pltpu.roll(x, 1, 0))
    xp1 = jnp.where(row == n - 1, next_row, pltpu.roll(x, n - 1, 0))
    return xm1, xp1


def _halo_specs(width, blk_fn):
    per = BLK // 8

    def prev_map(*idx):
        b, blk = blk_fn(*idx)
        return (b, jnp.maximum(blk * per - 1, 0), 0)

    def next_map(nrow8):
        def f(*idx):
            b, blk = blk_fn(*idx)
            return (b, jnp.minimum((blk + 1) * per, nrow8 - 1), 0)
        return f

    return (lambda: pl.BlockSpec((1, 8, width), prev_map),
            lambda nrow8: pl.BlockSpec((1, 8, width), next_map(nrow8)))


def _stream_edges(blk, nbc, nb):
    has_prev = jnp.logical_and(blk != 0, blk != nbc)
    has_next = jnp.logical_and(blk != nbc - 1, blk != nb - 1)
    return has_prev.astype(F32), has_next.astype(F32)


def _ssd_stages(u_ref, up_ref, un_ref, cw_ref, cb_ref, dtb_ref, aneg_ref, exp_ref, tri_ref,
                y_ref, xs_ref, h_ref, xdt_s, bm_s, cm_s, da_s, *, nbc, nb):
    d = pl.program_id(1)
    i = pl.program_id(2)
    blk = _block_of_step(d, i, nbc, nb)

    @pl.when(i == 0)
    def _():
        h_ref[...] = jnp.zeros_like(h_ref)

    has_prev, has_next = _stream_edges(blk, nbc, nb)
    u = u_ref[0]
    xbc = u[:, 256:768]
    prev_row = up_ref[0, 7:8, 256:768] * has_prev
    next_row = un_ref[0, 0:1, 256:768] * has_next
    xm1, xp1 = _shifted(xbc, prev_row, next_row)
    cw = cw_ref[...]
    act = _silu(cw[0:1] * xm1 + cw[1:2] * xbc + cw[2:3] * xp1 + cb_ref[...])
    xs = act[:, 0:256]
    xs_ref[0, 0] = xs
    dt = _softplus(_mm_mask_rhs(u[:, 768:896], exp_ref[...].astype(BF16)) + dtb_ref[0])
    xdt_s[...] = xs * dt
    da_s[...] = dt * aneg_ref[0]
    bm_s[...] = act[:, 256:384]
    cm_s[...] = act[:, 384:512]
    yield

    tri = tri_ref[d]
    tri_b = tri.astype(BF16)

    per_g = SSM_H // SSM_G
    per_l = LANE // SSM_P
    rows_of, cgs, bgs, bgt, xdh, xh, cum_col, cum_row, dec = [], [], [], [], [], [], [], [], []
    for j in range(NSUB):
        jj = jnp.where(d == 0, j, NSUB - 1 - j)
        rows = pl.ds(pl.multiple_of(jj * CHUNK, CHUNK), CHUNK)
        rows_of.append(rows)
        da = da_s[rows, :]
        xdt = xdt_s[rows, :]
        bm = bm_s[rows, :]
        cm = cm_s[rows, :]
        cum = _mm_mask_lhs(tri_b, da)
        tot = jnp.sum(da, axis=0, keepdims=True)
        cum_t = [cum[:, LANE * c:LANE * (c + 1)].T for c in range(HEAD_W // LANE)]
        bm_t = bm.T.astype(BF16)
        for h in range(SSM_H):
            hs = slice(SSM_P * h, SSM_P * (h + 1))
            gs = slice(SSM_N * (h // per_g), SSM_N * (h // per_g + 1))
            cgs.append(cm[:, gs])
            bgs.append(bm[:, gs])
            bgt.append(bm_t[gs, :])
            xdh.append(xdt[:, hs])
            cum_row.append(cum_t[h // per_l][SSM_P * (h % per_l):SSM_P * (h % per_l) + 1, :])
            xh.append(xdt[:, hs].astype(BF16))
            cum_col.append(cum[:, hs])
            dec.append(tot[:, hs])
        yield
    cb = [_dg(c.astype(BF16), b.astype(BF16), _NT) for c, b in zip(cgs[::per_g], bgs[::per_g])]
    seg = [jnp.where(tri > 0, jnp.exp(c - r), 0.0) for c, r in zip(cum_col, cum_row)]
    yield
    y_intra = [_mm(cb[n // per_g] * s, x) for n, (s, x) in enumerate(zip(seg, xh))]
    contrib = [_dg(b, (x * jnp.exp(t - c)).astype(BF16), _NN) for b, t, c, x in zip(bgt, dec, cum_col, xdh)]
    yield
    hp = [h_ref[h] for h in range(SSM_H)]
    h_prev = []
    for j in range(NSUB):
        for h in range(SSM_H):
            n = j * SSM_H + h
            h_prev.append(hp[h])
            hp[h] = hp[h] * jnp.exp(dec[n]) + contrib[n]
    for h in range(SSM_H):
        h_ref[h] = hp[h]
    for j in range(NSUB):
        for h in range(SSM_H):
            n = j * SSM_H + h
            y_ref[0, 0, rows_of[j], SSM_P * h:SSM_P * (h + 1)] = (
                y_intra[n] + _mm(cgs[n] * jnp.exp(cum_col[n]), h_prev[n]))


def _rwkv_prep_kernel(u_ref, up_ref, un_ref, mu_ref, a0_ref, wa_ref, wg_ref, kk_ref, ka_ref, bd_ref,
                      r_ref, k_ref, v_ref, g_ref, kkn_ref, ab_ref, tlw_ref, *, nbc, nb):
    blk = pl.program_id(1)
    has_prev, has_next = _stream_edges(blk, nbc, nb)
    u = u_ref[0]
    xm1, xp1 = _shifted(u, up_ref[0, 7:8, :] * has_prev, un_ref[0, 0:1, :] * has_next)
    mu = mu_ref[...]
    u = u + mu[0:1] * (xm1 - u) + mu[1:2] * (xp1 - u)
    r = u[:, 0:256]
    k = u[:, 256:512]
    lwa = u[:, 768:896]
    a = _sigmoid(a0_ref[...] + _mm(lwa, wa_ref[...]))
    g = _mm(_sigmoid(u[:, 896:1024]), wg_ref[...])
    kk = k * kk_ref[...]
    ss = _seg_sum(kk * kk, bd_ref[...].astype(BF16))
    kk = kk / jnp.maximum(jnp.sqrt(ss), 1e-12)
    r_ref[0] = r
    k_ref[0] = k * (1.0 + (a - 1.0) * ka_ref[...])
    v_ref[0] = u[:, 512:768]
    g_ref[0] = g
    kkn_ref[0] = kk
    ab_ref[0] = kk * a
    tlw_ref[0] = jnp.tanh(lwa)


def _rwkv_prep(u, mu, a0, wa, wg, k_k, k_a, bd64, nbc):
    bsz, t, _ = u.shape
    nb = t // BLK
    prev_spec, next_spec = _halo_specs(RWKV_W, lambda b, i: (b, i))
    full = lambda shape: pl.BlockSpec(shape, lambda b, i: (0,) * len(shape))
    tok = lambda w_: pl.BlockSpec((1, BLK, w_), lambda b, i: (b, i, 0))
    return pl.pallas_call(
        functools.partial(_rwkv_prep_kernel, nbc=nbc, nb=nb),
        grid=(bsz, nb),
        in_specs=[tok(RWKV_W), prev_spec(), next_spec(t // 8),
                  full((2, RWKV_W)), full((1, HEAD_W)), full((LANE, HEAD_W)), full((LANE, HEAD_W)),
                  full((1, HEAD_W)), full((1, HEAD_W)), full((HEAD_W, HEAD_W))],
        out_specs=[tok(HEAD_W)] * 6 + [tok(LANE)],
        out_shape=[jax.ShapeDtypeStruct((bsz, t, HEAD_W), F32)] * 6
        + [jax.ShapeDtypeStruct((bsz, t, LANE), F32)],
        compiler_params=_cparams(("arbitrary", "arbitrary")),
        name="rwkv_prep",
    )(u, u, u, mu, a0, wa, wg, k_k, k_a, bd64)


def _rwkv_stages(r_ref, k_ref, v_ref, kk_ref, ab_ref, tlw_ref, w0_ref, wdec_ref, tri_ref, stri_ref,
                 bd16_ref, eye_ref, y_ref, h_ref):
    d = pl.program_id(1)
    i = pl.program_id(2)

    @pl.when(i == 0)
    def _():
        h_ref[...] = jnp.zeros_like(h_ref)

    incl2 = tri_ref[d]
    strict2 = stri_ref[d]
    incl = incl2[:, :CHUNK]
    strict = strict2[:, :CHUNK]
    incl_b = incl.astype(BF16)
    bd16 = bd16_ref[...]
    eye = eye_ref[...]
    w0 = w0_ref[0]
    wdec = wdec_ref[0].astype(BF16)

    mask2 = jnp.concatenate([strict, incl], axis=0)
    stack = lambda a, b: jnp.concatenate([a, b], axis=0)
    heads = [slice(RWKV_N * h, RWKV_N * (h + 1)) for h in range(RWKV_H)]
    bf = lambda x: x.astype(BF16)

    rows_of, ar, rt, bt, kt, vh, bh, kh, pc = [], [], [], [], [], [], [], [], []
    for j in range(NSUB):
        jj = jnp.where(d == 0, j, NSUB - 1 - j)
        rows = pl.ds(pl.multiple_of(jj * CHUNK, CHUNK), CHUNK)
        rows_of.append(rows)
        wr = w0 + _mm(tlw_ref[0, rows, :], wdec)
        lw = -jnp.exp(-_softplus(-wr) - 0.5)
        gc = _mm_mask_lhs(incl_b, lw)
        tot = jnp.sum(lw, axis=0, keepdims=True)
        eng = jnp.exp(-gc)
        e_end = jnp.exp(tot - gc)
        abv = ab_ref[0, rows, :]
        k2 = k_ref[0, rows, :]
        at_j = bf(-kk_ref[0, rows, :] * jnp.exp(gc - lw))
        rt_f = r_ref[0, rows, :] * jnp.exp(gc)
        rt_j = bf(rt_f)
        bt_j = bf(abv * eng)
        kt_j = bf(k2 * eng)
        v_j = bf(v_ref[0, rows, :])
        bh_j = bf(abv * e_end)
        kh_j = bf(k2 * e_end)
        pc_j = jnp.exp(tot)
        for hs in heads:
            ar.append(stack(at_j[:, hs], rt_j[:, hs]))
            rt.append(rt_f[:, hs])
            bt.append(stack(bt_j[:, hs], bt_j[:, hs]))
            kt.append(kt_j[:, hs])
            vh.append(v_j[:, hs])
            bh.append(bh_j[:, hs])
            kh.append(kh_j[:, hs])
            pc.append(pc_j[:, hs])
        yield

    g1 = [_dg(a, b, _NT) for a, b in zip(ar, bt)]
    g2 = [_dg(a, b, _NT) for a, b in zip(ar, kt)]
    yield
    a_ab = [g[:CHUNK] * strict2 for g in g1]
    a_rb = [bf(g[CHUNK:, :CHUNK] * incl) for g in g1]
    avr = [_dg(bf(g * mask2), v, _NN) for g, v in zip(g2, vh)]
    khv = [_dg(v, k, _TN) for v, k in zip(vh, kh)]
    yield
    ad = [a * bd16 for a in a_ab]
    ee = [_parts(a - b) for a, b in zip(a_ab, ad)]
    q = [eye + a for a in ad]
    pw = [_kdot(x, x) for x in map(_parts, ad)]
    yield
    for _ in range(2):
        res = [_kdot(_parts(stack(a, x)), _parts(a)) for a, x in zip(pw, q)]
        pw = [r[:CHUNK] for r in res]
        q = [x + r[CHUNK:] for x, r in zip(q, res)]
        yield
    p = [x + _kdot(_parts(x), _parts(a)) for x, a in zip(q, pw)]
    yield
    p_s = [_parts(x) for x in p]
    f = [_parts(_kdot(x, e)) for x, e in zip(p_s, ee)]
    yield
    f2 = [_parts(_kdot(x, x)) for x in f]
    t1 = [x + _kdot(a, xs) for x, a, xs in zip(p, f, p_s)]
    yield
    tinv = [bf((x + _kdot(a, _parts(x)))[:, :CHUNK]) for x, a in zip(t1, f2)]
    z = [jnp.concatenate([a[:CHUNK], bf(c[:CHUNK])], axis=1) for a, c in zip(ar, avr)]
    yield
    tz = [bf(_dg(t, x, _NN)) for t, x in zip(tinv, z)]
    w = [bf(_dg(t, b, _TN)) for t, b in zip(tinv, bh)]
    yield
    rz = [_dg(b, x, _NN) for b, x in zip(a_rb, tz)]
    mn = [_dg(x, y, _TN) for x, y in zip(z, w)]
    qm = [bf(r + x[:, :CHUNK]) for r, x in zip(rt, rz)]
    y0 = [c[CHUNK:] + x[:, CHUNK:] for c, x in zip(avr, rz)]
    m1 = [bf(x[:CHUNK]) for x in mn]
    n1 = [x[CHUNK:] + e for x, e in zip(mn, khv)]
    yield

    ht = [h_ref[h] for h in range(RWKV_H)]
    for j in range(NSUB):
        for h in range(RWKV_H):
            n = j * RWKV_H + h
            hb = bf(ht[h])
            y_ref[0, 0, rows_of[j], heads[h]] = _dg(qm[n], hb, _NT) + y0[n]
            ht[h] = ht[h] * pc[n] + _dg(hb, m1[n], _NN) + n1[n]
        yield
    for h in range(RWKV_H):
        h_ref[h] = ht[h]


def _interleave(main, side):
    k = 0
    for _ in main:
        for _ in range(len(side)):
            gen = side[k % len(side)]
            k += 1
            if next(gen, StopIteration) is not StopIteration:
                break
    for gen in side:
        for _ in gen:
            pass


def _scan_kernel(ug_ref, gwd_ref, gbd_ref, us_ref, up_ref, un_ref, cw_ref, cb_ref, dtb_ref, aneg_ref, exp_ref,
                 r_ref, k_ref, v_ref, kk_ref, ab_ref, tlw_ref, w0_ref, wdec_ref,
                 tri_ref, tri2_ref, stri2_ref, bd16_ref, eye_ref,
                 gla_o_ref, ssd_y_ref, ssd_xs_ref, rw_y_ref,
                 gla_s, ssd_h, xdt_s, bm_s, cm_s, da_s, rw_h, *, nbc, nb):
    rwkv = _rwkv_stages(r_ref, k_ref, v_ref, kk_ref, ab_ref, tlw_ref, w0_ref, wdec_ref, tri2_ref, stri2_ref,
                        bd16_ref, eye_ref, rw_y_ref, rw_h)
    ssd = _ssd_stages(us_ref, up_ref, un_ref, cw_ref, cb_ref, dtb_ref, aneg_ref, exp_ref, tri_ref,
                      ssd_y_ref, ssd_xs_ref, ssd_h, xdt_s, bm_s, cm_s, da_s, nbc=nbc, nb=nb)
    gla = _gla_stages(ug_ref, gwd_ref, gbd_ref, tri_ref, gla_o_ref, gla_s)
    _interleave(rwkv, [ssd, gla])


def _scans(u_gla, gla_wd, gla_bd, u_ssm, conv_w, conv_b, dtb, aneg, expand,
           r, k2, v, kk, ab, tlw, rw_w0, rw_wd, tri, stri, bd16, eye, nbc):
    bsz, t, _ = r.shape
    nb = t // BLK
    blk = lambda b, d, i: _block_of_step(d, i, nbc, nb)
    tok = lambda w_: pl.BlockSpec((1, BLK, w_), lambda b, d, i: (b, blk(b, d, i), 0))
    full = lambda shape: pl.BlockSpec(shape, lambda b, d, i: (0,) * len(shape))
    by_dir = lambda *shape: pl.BlockSpec((1,) + shape, lambda b, d, i: (d,) + (0,) * len(shape))
    prev_spec, next_spec = _halo_specs(SSM_W, lambda b, d, i: (b, blk(b, d, i)))
    dup = lambda m: jnp.concatenate([m, m], axis=-1)
    out = pl.BlockSpec((1, 1, BLK, HEAD_W), lambda b, d, i: (d, b, blk(b, d, i), 0))
    return pl.pallas_call(
        functools.partial(_scan_kernel, nbc=nbc, nb=nb),
        grid=(bsz, 2, nb),
        in_specs=[tok(GLA_W), by_dir(LANE, LANE), by_dir(1, LANE),
                  tok(SSM_W), prev_spec(), next_spec(t // 8), full((3, 512)), full((1, 512)),
                  by_dir(1, HEAD_W), by_dir(1, HEAD_W), full((LANE, HEAD_W))]
        + [tok(HEAD_W)] * 5 + [tok(LANE), by_dir(1, HEAD_W), by_dir(LANE, HEAD_W),
                               full((2, CHUNK, CHUNK)), full((2, CHUNK, LANE)), full((2, CHUNK, LANE)),
                               full((CHUNK, LANE)), full((CHUNK, LANE))],
        out_specs=[out] * 4,
        out_shape=[jax.ShapeDtypeStruct((2, bsz, t, HEAD_W), F32)] * 4,
        scratch_shapes=[pltpu.VMEM((GLA_H, GLA_DV, GLA_DK), F32), pltpu.VMEM((SSM_H, SSM_N, SSM_P), F32),
                        pltpu.VMEM((BLK, HEAD_W), F32), pltpu.VMEM((BLK, LANE), F32),
                        pltpu.VMEM((BLK, LANE), F32), pltpu.VMEM((BLK, HEAD_W), F32),
                        pltpu.VMEM((RWKV_H, RWKV_N, RWKV_N), F32)],
        compiler_params=_cparams(("arbitrary", "arbitrary", "arbitrary")),
        name="scans",
    )(u_gla, gla_wd, gla_bd, u_ssm, u_ssm, u_ssm, conv_w, conv_b, dtb, aneg, expand,
      r, k2, v, kk, ab, tlw, rw_w0, rw_wd, tri, dup(tri), dup(stri), dup(bd16), dup(eye))


def _rope(x, c, sa, sb):
    w = x.shape[1]
    return x * c + pltpu.roll(x, w - ROPE_AXIS_DIM // 2, 1) * sa + pltpu.roll(x, ROPE_AXIS_DIM // 2, 1) * sb


def _att_prep_kernel(u_ref, c_ref, sa_ref, sb_ref, qg_ref, kg_ref, bd_ref, q_ref, k_ref, vt_ref):
    u = u_ref[0]
    bd = bd_ref[...].astype(BF16)
    c, sa, sb = c_ref[...], sa_ref[...], sb_ref[...]
    c2 = jnp.concatenate([c, c], axis=1)
    sa2 = jnp.concatenate([sa, sa], axis=1)
    sb2 = jnp.concatenate([sb, sb], axis=1)
    q = u[:, 0:256]
    k = u[:, 256:384]
    qms = _seg_sum(q * q, bd) * (1.0 / ATT_HD)
    kms = _seg_sum(k * k, bd[:LANE, :LANE]) * (1.0 / ATT_HD)
    qn = q * lax.rsqrt(qms + EPS) * qg_ref[...]
    kn = k * lax.rsqrt(kms + EPS) * kg_ref[...]
    q_ref[0] = _rope(qn, c2, sa2, sb2) * (ATT_HD ** -0.5)
    kr = _rope(kn, c, sa, sb).astype(BF16)
    k_ref[0, 0] = kr[:, :ATT_HD]
    k_ref[0, 1] = kr[:, ATT_HD:]
    vt = u[:, 384:512].T.astype(BF16)
    vt_ref[0, 0] = vt[:ATT_HD]
    vt_ref[0, 1] = vt[ATT_HD:]


def _att_prep(u, c, sa, sb, qg, kg, bd64):
    bsz, t, _ = u.shape
    nb = t // BLK
    full = lambda shape: pl.BlockSpec(shape, lambda b, i: (0,) * len(shape))
    tab = pl.BlockSpec((BLK, LANE), lambda b, i: (i, 0))
    return pl.pallas_call(
        _att_prep_kernel,
        grid=(bsz, nb),
        in_specs=[pl.BlockSpec((1, BLK, ATT_W), lambda b, i: (b, i, 0)), tab, tab, tab,
                  full((1, HEAD_W)), full((1, LANE)), full((HEAD_W, HEAD_W))],
        out_specs=[pl.BlockSpec((1, BLK, HEAD_W), lambda b, i: (b, i, 0)),
                   pl.BlockSpec((1, ATT_HKV, BLK, ATT_HD), lambda b, i: (b, 0, i, 0)),
                   pl.BlockSpec((1, ATT_HKV, ATT_HD, BLK), lambda b, i: (b, 0, 0, i))],
        out_shape=[jax.ShapeDtypeStruct((bsz, t, HEAD_W), F32),
                   jax.ShapeDtypeStruct((bsz, ATT_HKV, t, ATT_HD), BF16),
                   jax.ShapeDtypeStruct((bsz, ATT_HKV, ATT_HD, t), BF16)],
        compiler_params=_cparams(("arbitrary", "arbitrary")),
        name="att_prep",
    )(u, c, sa, sb, qg, kg, bd64)


def _att_kernel(q_ref, k_ref, vt_ref, o_ref, *, nbc, n_ctx):
    i = pl.program_id(1)
    per_kv = ATT_HQ // ATT_HKV

    def attend(n_keys):
        qt = q_ref[0].T.astype(BF16)
        heads = range(ATT_HQ)
        s = [jnp.dot(k_ref[0, h // per_kv, :n_keys, :], qt[h * ATT_HD:(h + 1) * ATT_HD],
                     preferred_element_type=F32) for h in heads]
        e = [jnp.exp(x - jnp.max(x, axis=0, keepdims=True)) for x in s]
        l = [jnp.sum(x, axis=0, keepdims=True) for x in e]
        for h in heads:
            ot = jnp.dot(vt_ref[0, h // per_kv, :, :n_keys], e[h].astype(BF16), preferred_element_type=F32) / l[h]
            o_ref[0, :, h * ATT_HD:(h + 1) * ATT_HD] = ot.T

    @pl.when(i < nbc)
    def _():
        attend(n_ctx)

    @pl.when(i >= nbc)
    def _():
        attend(k_ref.shape[2])


def _attention(q, k, vt, nbc, n_ctx):
    bsz, t, _ = q.shape
    nb = t // BLK
    return pl.pallas_call(
        functools.partial(_att_kernel, nbc=nbc, n_ctx=n_ctx),
        grid=(bsz, nb),
        in_specs=[pl.BlockSpec((1, BLK, HEAD_W), lambda b, i: (b, i, 0)),
                  pl.BlockSpec((1, ATT_HKV, t, ATT_HD), lambda b, i: (b, 0, 0, 0)),
                  pl.BlockSpec((1, ATT_HKV, ATT_HD, t), lambda b, i: (b, 0, 0, 0))],
        out_specs=pl.BlockSpec((1, BLK, HEAD_W), lambda b, i: (b, i, 0)),
        out_shape=jax.ShapeDtypeStruct((bsz, t, HEAD_W), F32),
        compiler_params=_cparams(("arbitrary", "arbitrary")),
        name="attention",
    )(q, k, vt)


def _out_kernel(h_ref, mod_ref, glaf_ref, glab_ref, glag_ref, ssdf_ref, ssdb_ref, xs_ref, z_ref,
                rwf_ref, rwb_ref, r_ref, k_ref, v_ref, g_ref, att_ref,
                glan_ref, ssd_d_ref, ssdn_ref, rk_ref, lng_ref, lnb_ref, bd_ref, w_ref, o_ref):
    bd = bd_ref[...].astype(BF16)
    seg_mean = lambda x: _seg_sum(x, bd) * (1.0 / 64.0)
    o = glaf_ref[0, 0] + glab_ref[0, 0]
    y_gla = o * lax.rsqrt(seg_mean(o * o) + EPS) * glan_ref[...] * _silu(glag_ref[0])
    y = ssdf_ref[0, 0] + ssdb_ref[0, 0] + ssd_d_ref[...] * xs_ref[0, 0]
    y = y * _silu(z_ref[0])
    y_ssd = y * lax.rsqrt(jnp.mean(y * y, axis=-1, keepdims=True) + EPS) * ssdn_ref[...]
    y = rwf_ref[0, 0] + rwb_ref[0, 0]
    mu = seg_mean(y)
    yc = y - mu
    var = seg_mean(yc * yc)
    yn = yc * lax.rsqrt(var + RWKV_GN_EPS) * lng_ref[...] + lnb_ref[...]
    v = v_ref[0]
    bonus = _seg_sum(r_ref[0] * k_ref[0] * rk_ref[...], bd) * v
    y_rw = (yn + bonus) * g_ref[0]
    w = w_ref[...]
    proj = (_mm(y_gla, w[0:256]) + _mm(y_ssd, w[256:512])) + (_mm(y_rw, w[512:768]) + _mm(att_ref[0], w[768:1024]))
    o_ref[0] = h_ref[0] + mod_ref[0][5:6] * proj


def _mix_out(h, mod, gla_o, u_gla, ssd_y, ssd_xs, u_ssm, rw_y, r, k2, v, g, att_o,
             gla_n, ssd_d, ssd_n, r_k, ln_g, ln_b, bd64, w_out, layer, nbc):
    bsz, t, d = h.shape
    tok = lambda w_, col=0: pl.BlockSpec((1, BLK, w_), lambda b, i: (b, i, col))
    dirn = lambda dd: pl.BlockSpec((1, 1, BLK, HEAD_W), lambda b, i: (dd, b, i, 0))
    full = lambda shape: pl.BlockSpec(shape, lambda b, i: (0,) * len(shape))
    vec = full((1, HEAD_W))
    return pl.pallas_call(
        _out_kernel,
        grid=(bsz, t // BLK),
        in_specs=[tok(d), _mod_spec(layer, nbc, d),
                  dirn(0), dirn(1), tok(HEAD_W, 2),
                  dirn(0), dirn(1), dirn(0), tok(HEAD_W, 0),
                  dirn(0), dirn(1), tok(HEAD_W), tok(HEAD_W), tok(HEAD_W), tok(HEAD_W), tok(HEAD_W),
                  vec, vec, vec, vec, vec, vec, full((HEAD_W, HEAD_W)),
                  pl.BlockSpec((None, d, d), lambda b, i: (layer, 0, 0))],
        out_specs=tok(d),
        out_shape=jax.ShapeDtypeStruct(h.shape, F32),
        compiler_params=_cparams(("arbitrary", "arbitrary")),
        name="mix_out",
    )(h, mod, gla_o, gla_o, u_gla, ssd_y, ssd_y, ssd_xs, u_ssm, rw_y, rw_y, r, k2, v, g, att_o,
      gla_n, ssd_d, ssd_n, r_k, ln_g, ln_b, bd64, w_out)


def _rope_tables(n_ctx, n_lat):
    rows = n_lat // GRID_W
    row = jnp.repeat(jnp.arange(rows, dtype=F32), GRID_W)
    col = jnp.tile(jnp.arange(GRID_W, dtype=F32), rows)
    inv = ROPE_THETA ** (-jnp.arange(0, ROPE_AXIS_DIM, 2, dtype=F32) / ROPE_AXIS_DIM)
    ang = jnp.stack([row[:, None] * inv, col[:, None] * inv], axis=1)
    cos, sin = jnp.cos(ang), jnp.sin(ang)
    zero = jnp.zeros_like(sin)
    c = jnp.concatenate([cos, cos], axis=-1).reshape(n_lat, ATT_HD)
    sa = jnp.concatenate([-sin, zero], axis=-1).reshape(n_lat, ATT_HD)
    sb = jnp.concatenate([zero, sin], axis=-1).reshape(n_lat, ATT_HD)
    pad = lambda x, fill: jnp.concatenate([jnp.full((n_ctx, ATT_HD), fill, F32), x], axis=0)
    two = lambda x: jnp.concatenate([x, x], axis=1)
    return two(pad(c, 1.0)), two(pad(sa, 0.0)), two(pad(sb, 0.0))


def _masks():
    t = jnp.arange(CHUNK)
    lower = (t[None, :] <= t[:, None]).astype(F32)
    slower = (t[None, :] < t[:, None]).astype(F32)
    tri = jnp.stack([lower, lower.T])
    stri = jnp.stack([slower, slower.T])
    bd16 = (t[None, :] // SUB == t[:, None] // SUB).astype(F32)
    eye = jnp.eye(CHUNK, dtype=F32)
    c = jnp.arange(HEAD_W)
    bd64 = (c[None, :] // 64 == c[:, None] // 64).astype(F32)
    expand = (jnp.arange(LANE)[:, None] == c[None, :] // 64).astype(F32)
    return tri, stri, bd16, eye, bd64, expand


def _pad_cols(x, width):
    return jnp.pad(x, [(0, 0)] * (x.ndim - 1) + [(0, width - x.shape[-1])])


def _pad_rows(x, height, top=0):
    return jnp.pad(x, [(0, 0)] * (x.ndim - 2) + [(top, height - top - x.shape[-2]), (0, 0)])


def _rep(x, n):
    return jnp.repeat(x, n, axis=-1)


def kernel(x, c, ctx, c_ctx, norm_g, w_mod, b_mod, ffn_in, ffn_out, w_in, w_out, gla_w_dec, gla_b_dec, gla_norm, ssm_conv_w, ssm_conv_b, ssm_dt_bias, ssm_a_log, ssm_d, ssm_norm, rwkv_shift_mu, rwkv_w0, rwkv_w_dec, rwkv_a0, rwkv_w_a, rwkv_w_g, rwkv_k_k, rwkv_k_a, rwkv_r_k, rwkv_ln_g, rwkv_ln_b, att_q_norm, att_k_norm):
    bsz, n_lat, d = x.shape
    n_ctx = ctx.shape[1]
    depth = w_mod.shape[0]
    assert bsz == 4 and n_ctx % BLK == 0 and n_lat % BLK == 0 and n_lat % GRID_W == 0
    nbc = n_ctx // BLK
    tri, stri, bd16, eye, bd64, expand = _masks()
    rope_c, rope_sa, rope_sb = _rope_tables(n_ctx, n_lat)

    cvec = jnp.concatenate([c, c_ctx[None], jnp.zeros((8 - bsz - 1, d), F32)], axis=0)
    mod_all = _compute_mod(cvec, w_mod, b_mod).reshape(depth, 8, N_MOD, d)

    w1 = ffn_in.astype(BF16)
    w2 = ffn_out.astype(BF16)
    wo = w_out.astype(BF16)
    o_ssm = 784
    o_rw = o_ssm + 772
    o_att = o_rw + 1024
    w_proj = jnp.concatenate([_pad_cols(w_in[..., :o_ssm], GLA_W), _pad_cols(w_in[..., o_ssm:o_rw], SSM_W),
                              w_in[..., o_rw:o_att], w_in[..., o_att:]], axis=-1).astype(BF16)

    h = x
    for l in range(depth):
        gla_wd = _pad_rows(gla_w_dec[l], LANE)
        gla_bd = gla_b_dec[l][:, None, :]
        ssm_dtb = _rep(ssm_dt_bias[l], SSM_P)[:, None, :]
        ssm_an = _rep(-jnp.exp(ssm_a_log[l]), SSM_P)[:, None, :]
        rw_wd = _pad_rows(rwkv_w_dec[l], LANE)
        rw_wa = _pad_rows(rwkv_w_a[l], LANE, top=64)
        rw_w0 = rwkv_w0[l][:, None, :]

        h = _ffn(h, mod_all, norm_g[l, 0][None], w1, w2, l, 0, nbc, ctx=ctx if l == 0 else None)
        u_gla, u_ssm, u_rw, u_att = _inproj(h, mod_all, norm_g[l, 1][None], w_proj, l, nbc)

        r, k2, v, g, kk, ab, tlw = _rwkv_prep(u_rw, rwkv_shift_mu[l], rwkv_a0[l][None], rw_wa, rwkv_w_g[l],
                                              rwkv_k_k[l][None], rwkv_k_a[l][None], bd64, nbc)
        gla_o, ssd_y, ssd_xs, rw_y = _scans(u_gla, gla_wd, gla_bd, u_ssm, ssm_conv_w[l], ssm_conv_b[l][None], ssm_dtb,
                                            ssm_an, expand, r, k2, v, kk, ab, tlw, rw_w0, rw_wd, tri, stri, bd16,
                                            eye, nbc)
        qn, kt, vt = _att_prep(u_att, rope_c, rope_sa, rope_sb, jnp.tile(att_q_norm[l], ATT_HQ)[None],
                               jnp.tile(att_k_norm[l], ATT_HKV)[None], bd64)
        att_o = _attention(qn, kt, vt, nbc, n_ctx)

        h = _mix_out(h, mod_all, gla_o, u_gla, ssd_y, ssd_xs, u_ssm, rw_y, r, k2, v, g, att_o,
                     jnp.tile(gla_norm[l], GLA_H)[None], _rep(ssm_d[l], SSM_P)[None], ssm_norm[l][None],
                     rwkv_r_k[l].reshape(1, HEAD_W), rwkv_ln_g[l][None], rwkv_ln_b[l][None], bd64,
                     wo, l, nbc)
        h = _ffn(h, mod_all, norm_g[l, 2][None], w1, w2, l, 1, nbc, latent_only=l == depth - 1)
    return h
```

```python
import functools
import math

import jax
import jax.numpy as jnp
from jax import lax
from jax.experimental import pallas as pl
from jax.experimental.pallas import tpu as pltpu

F32 = jnp.float32
BF16 = jnp.bfloat16

N_MOD = 9
EPS = 1e-6
GLA_H, GLA_DK, GLA_DV, GLA_LORA, GLA_TAU = 4, 32, 64, 16, 16.0
SSM_H, SSM_P, SSM_G, SSM_N = 4, 64, 2, 64
RWKV_H, RWKV_N, RWKV_GN_EPS = 4, 64, 64e-5
ATT_HQ, ATT_HKV, ATT_HD = 4, 2, 64
GRID_W = 64
ROPE_THETA = 10000.0
ROPE_AXIS_DIM = ATT_HD // 2

CHUNK = 64
SUB = 16
BLK = 256
NSUB = BLK // CHUNK
LANE = 128
HEAD_W = 256
VMEM_LIMIT = 56 * 1024 * 1024

GLA_W = 896
SSM_W = 896
RWKV_W = 1024
ATT_W = 512


def _cparams(sem):
    return pltpu.CompilerParams(dimension_semantics=sem, vmem_limit_bytes=VMEM_LIMIT)


def _mm(a, b):
    return jnp.dot(a.astype(BF16), b.astype(BF16), preferred_element_type=F32)


def _dg(a, b, dims):
    return lax.dot_general(a, b, (dims, ((), ())), preferred_element_type=F32)


_NN = ((1,), (0,))
_NT = ((1,), (1,))
_TN = ((0,), (0,))


def _split2(x):
    hi = x.astype(BF16)
    lo = (x - hi.astype(F32)).astype(BF16)
    return hi, lo


def _split3(x):
    hi = x.astype(BF16)
    r1 = x - hi.astype(F32)
    mid = r1.astype(BF16)
    lo = (r1 - mid.astype(F32)).astype(BF16)
    return hi, mid, lo


_parts = _split2


def _kdot(a, b):
    ah, al = a
    bh, bl = b
    return _dg(jnp.concatenate([ah, al], axis=1), jnp.concatenate([bh, bl, bh, bl], axis=0), _NN)


def _mm_mask_lhs(mask_bf16, x, dims=_NN):
    hi, mid, lo = _split3(x)
    return _dg(mask_bf16, hi, dims) + (_dg(mask_bf16, mid, dims) + _dg(mask_bf16, lo, dims))


def _mm_mask_rhs(x, mask_bf16, dims=_NN):
    hi, mid, lo = _split3(x)
    return _dg(hi, mask_bf16, dims) + (_dg(mid, mask_bf16, dims) + _dg(lo, mask_bf16, dims))


def _seg_sum(x, mask_bf16):
    hi, lo = _split2(x)
    return _dg(hi, mask_bf16, _NN) + _dg(lo, mask_bf16, _NN)


def _sigmoid(x):
    return jax.nn.sigmoid(x)


def _silu(x):
    return x * jax.nn.sigmoid(x)


def _softplus(x):
    return jnp.maximum(x, 0.0) + jnp.log1p(jnp.exp(-jnp.abs(x)))


def _modulate(x, g, shift, scale):
    ms = jnp.mean(x * x, axis=-1, keepdims=True)
    return (x * lax.rsqrt(ms + EPS) * g) * (1.0 + scale) + shift


def _block_of_step(d, i, nbc, nb):
    back = jnp.where(i < nbc, nbc - 1 - i, nb + nbc - 1 - i)
    return jnp.where(d == 0, i, back)


def _mod_row(b, i, nbc):
    return jnp.where(i < nbc, 4, b)


def _mod_kernel(c_ref, w_ref, b_ref, o_ref):
    o_ref[0] = _mm(_silu(c_ref[...]), w_ref[0]) + b_ref[0]


def _compute_mod(cvec, w_mod, b_mod):
    depth, d, n = w_mod.shape
    tn = 1024
    return pl.pallas_call(
        _mod_kernel,
        grid=(depth, n // tn),
        in_specs=[pl.BlockSpec((8, d), lambda l, j: (0, 0)),
                  pl.BlockSpec((1, d, tn), lambda l, j: (l, 0, j)),
                  pl.BlockSpec((1, 1, tn), lambda l, j: (l, 0, j))],
        out_specs=pl.BlockSpec((1, 8, tn), lambda l, j: (l, 0, j)),
        out_shape=jax.ShapeDtypeStruct((depth, 8, n), F32),
        compiler_params=_cparams(("arbitrary", "arbitrary")),
        name="adaln_mod",
    )(cvec, w_mod, b_mod.reshape(depth, 1, n))


def _ffn_math(x, m, g, w1_ref, w2_ref, i0, d_ff):
    hm = _modulate(x, g, m[i0:i0 + 1], m[i0 + 1:i0 + 2]).astype(BF16)
    ab = jnp.dot(hm, w1_ref[...], preferred_element_type=F32)
    a = ab[:, :d_ff]
    act = (_silu(a) * ab[:, d_ff:]).astype(BF16)
    y = jnp.dot(act, w2_ref[...], preferred_element_type=F32)
    return x + (0.5 * m[i0 + 2:i0 + 3]) * y


def _ffn_kernel(h_ref, mod_ref, g_ref, w1_ref, w2_ref, o_ref, *, i0, d_ff):
    o_ref[0] = _ffn_math(h_ref[0], mod_ref[0], g_ref[...], w1_ref, w2_ref, i0, d_ff)


def _ffn_first_kernel(ctx_ref, x_ref, mod_ref, g_ref, w1_ref, w2_ref, o_ref, *, i0, d_ff, nbc):
    x = jnp.where(pl.program_id(1) < nbc, ctx_ref[0], x_ref[0])
    o_ref[0] = _ffn_math(x, mod_ref[0], g_ref[...], w1_ref, w2_ref, i0, d_ff)


def _ffn_last_kernel(h_ref, mod_ref, g_ref, w1_ref, w2_ref, o_ref, *, i0, d_ff, nbc):
    @pl.when(pl.program_id(1) >= nbc)
    def _():
        o_ref[0] = _ffn_math(h_ref[0], mod_ref[0], g_ref[...], w1_ref, w2_ref, i0, d_ff)


def _mod_spec(layer, nbc, d):
    return pl.BlockSpec((None, 1, N_MOD, d), lambda b, i: (layer, _mod_row(b, i, nbc), 0, 0))


def _ffn(h, mod, g, w1, w2, layer, which, nbc, ctx=None, latent_only=False):
    bsz, _, d = h.shape
    d_ff = w2.shape[2]
    t = h.shape[1] + (0 if ctx is None else ctx.shape[1])
    tok = pl.BlockSpec((1, BLK, d), lambda b, i: (b, i, 0))
    lat = pl.BlockSpec((1, BLK, d), lambda b, i: (b, jnp.maximum(i - nbc, 0), 0))
    common = [_mod_spec(layer, nbc, d),
              pl.BlockSpec((1, d), lambda b, i: (0, 0)),
              pl.BlockSpec((None, None, d, 2 * d_ff), lambda b, i: (layer, which, 0, 0)),
              pl.BlockSpec((None, None, d_ff, d), lambda b, i: (layer, which, 0, 0))]
    if ctx is not None:
        body = functools.partial(_ffn_first_kernel, i0=6 * which, d_ff=d_ff, nbc=nbc)
        ins = [pl.BlockSpec((1, BLK, d), lambda b, i: (b, jnp.minimum(i, nbc - 1), 0)), lat]
        args = (ctx, h)
    else:
        body = functools.partial(_ffn_last_kernel if latent_only else _ffn_kernel, i0=6 * which, d_ff=d_ff,
                                 **({"nbc": nbc} if latent_only else {}))
        ins = [tok]
        args = (h,)
    return pl.pallas_call(
        body,
        grid=(bsz, t // BLK),
        in_specs=ins + common,
        out_specs=lat if latent_only else tok,
        out_shape=jax.ShapeDtypeStruct((bsz, t - nbc * BLK if latent_only else t, d), F32),
        compiler_params=_cparams(("arbitrary", "arbitrary")),
        name="ffn",
    )(*args, mod, g, w1, w2)


def _rope(x, c, sa, sb):
    w = x.shape[1]
    return x * c + pltpu.roll(x, w - ROPE_AXIS_DIM // 2, 1) * sa + pltpu.roll(x, ROPE_AXIS_DIM // 2, 1) * sb


def _att_prep(u, c, sa, sb, qg, kg, bd, q_ref, k_ref, vt_ref):
    c2 = jnp.concatenate([c, c], axis=1)
    sa2 = jnp.concatenate([sa, sa], axis=1)
    sb2 = jnp.concatenate([sb, sb], axis=1)
    q = u[:, 0:256]
    k = u[:, 256:384]
    qms = _seg_sum(q * q, bd) * (1.0 / ATT_HD)
    kms = _seg_sum(k * k, bd[:LANE, :LANE]) * (1.0 / ATT_HD)
    qn = q * lax.rsqrt(qms + EPS) * qg
    kn = k * lax.rsqrt(kms + EPS) * kg
    q_ref[0] = _rope(qn, c2, sa2, sb2) * (ATT_HD ** -0.5)
    kr = _rope(kn, c, sa, sb).astype(BF16)
    k_ref[0, 0] = kr[:, :ATT_HD]
    k_ref[0, 1] = kr[:, ATT_HD:]
    vt = u[:, 384:512].T.astype(BF16)
    vt_ref[0, 0] = vt[:ATT_HD]
    vt_ref[0, 1] = vt[ATT_HD:]


def _inproj_kernel(h_ref, hp_ref, hn_ref, mod_ref, g_ref, w_ref, c_ref, sa_ref, sb_ref, qg_ref, kg_ref, bd_ref,
                   mu_ref, a0_ref, wa_ref, wg_ref, kk_ref, ka_ref,
                   gla_ref, ssm_ref, q_ref, k_ref, vt_ref,
                   r_ref, k2_ref, v_ref, gg_ref, kkn_ref, ab_ref, tlw_ref, *, nbc, nb):
    m = mod_ref[0]
    g = g_ref[...]
    bd = bd_ref[...].astype(BF16)
    hm = _modulate(h_ref[0], g, m[3:4], m[4:5]).astype(BF16)
    u = jnp.dot(hm, w_ref[...], preferred_element_type=F32)
    gla_ref[0] = u[:, :GLA_W]
    ssm_ref[0] = u[:, GLA_W:GLA_W + SSM_W]
    o_rw = GLA_W + SSM_W
    _att_prep(u[:, o_rw + RWKV_W:], c_ref[...], sa_ref[...], sb_ref[...], qg_ref[...], kg_ref[...], bd,
              q_ref, k_ref, vt_ref)
    halo = jnp.concatenate([hp_ref[0], hn_ref[0]], axis=0)
    uh = jnp.dot(_modulate(halo, g, m[3:4], m[4:5]).astype(BF16), w_ref[:, o_rw:o_rw + RWKV_W],
                 preferred_element_type=F32)
    has_prev, has_next = _stream_edges(pl.program_id(1), nbc, nb)
    _rwkv_prep(u[:, o_rw:o_rw + RWKV_W], uh[7:8] * has_prev, uh[8:9] * has_next, mu_ref[...], a0_ref[...],
               wa_ref[...], wg_ref[...], kk_ref[...], ka_ref[...], bd,
               r_ref, k2_ref, v_ref, gg_ref, kkn_ref, ab_ref, tlw_ref)


def _inproj(h, mod, g, w, c, sa, sb, qg, kg, bd64, mu, a0, wa, wg, k_k, k_a, layer, nbc):
    bsz, t, d = h.shape
    nb = t // BLK
    widths = (GLA_W, SSM_W)
    full = lambda shape: pl.BlockSpec(shape, lambda b, i: (0,) * len(shape))
    tab = pl.BlockSpec((BLK, LANE), lambda b, i: (i, 0))
    tok = lambda w_: pl.BlockSpec((1, BLK, w_), lambda b, i: (b, i, 0))
    prev_spec, next_spec = _halo_specs(d, lambda b, i: (b, i))
    return pl.pallas_call(
        functools.partial(_inproj_kernel, nbc=nbc, nb=nb),
        grid=(bsz, nb),
        in_specs=[tok(d), prev_spec(), next_spec(t // 8),
                  _mod_spec(layer, nbc, d),
                  pl.BlockSpec((1, d), lambda b, i: (0, 0)),
                  pl.BlockSpec((None, d, sum(widths) + RWKV_W + ATT_W), lambda b, i: (layer, 0, 0)),
                  tab, tab, tab, full((1, HEAD_W)), full((1, LANE)), full((HEAD_W, HEAD_W)),
                  full((2, RWKV_W)), full((1, HEAD_W)), full((LANE, HEAD_W)), full((LANE, HEAD_W)),
                  full((1, HEAD_W)), full((1, HEAD_W))],
        out_specs=[tok(w_) for w_ in widths]
        + [tok(HEAD_W),
           pl.BlockSpec((1, ATT_HKV, BLK, ATT_HD), lambda b, i: (b, 0, i, 0)),
           pl.BlockSpec((1, ATT_HKV, ATT_HD, BLK), lambda b, i: (b, 0, 0, i))]
        + [tok(HEAD_W)] * 6 + [tok(LANE)],
        out_shape=[jax.ShapeDtypeStruct((bsz, t, w_), F32) for w_ in widths]
        + [jax.ShapeDtypeStruct((bsz, t, HEAD_W), F32),
           jax.ShapeDtypeStruct((bsz, ATT_HKV, t, ATT_HD), BF16),
           jax.ShapeDtypeStruct((bsz, ATT_HKV, ATT_HD, t), BF16)]
        + [jax.ShapeDtypeStruct((bsz, t, HEAD_W), F32)] * 6 + [jax.ShapeDtypeStruct((bsz, t, LANE), F32)],
        compiler_params=_cparams(("arbitrary", "arbitrary")),
        name="inproj",
    )(h, h, h, mod, g, w, c, sa, sb, qg, kg, bd64, mu, a0, wa, wg, k_k, k_a)


def _gla_stages(u_ref, wdec_ref, bdec_ref, tri_ref, o_ref, s_ref):
    d = pl.program_id(1)
    i = pl.program_id(2)

    @pl.when(i == 0)
    def _():
        s_ref[...] = jnp.zeros_like(s_ref)

    tri = tri_ref[d]
    tri_b = tri.astype(BF16)
    wdec = wdec_ref[0].astype(BF16)
    bdec = bdec_ref[0]

    rows_of, qh, kih, keh, vh, dec = [], [], [], [], [], []
    for j in range(NSUB):
        jj = jnp.where(d == 0, j, NSUB - 1 - j)
        rows = pl.ds(pl.multiple_of(jj * CHUNK, CHUNK), CHUNK)
        rows_of.append(rows)
        u = u_ref[0, rows, :]
        q = u[:, 0:128] * (GLA_DK ** -0.5)
        k = u[:, 128:256]
        v = u[:, 256:512]
        z = _mm(u[:, 768:896], wdec) + bdec
        la = -_softplus(-z) * (1.0 / GLA_TAU)
        bc = _mm_mask_lhs(tri_b, la)
        bt = jnp.sum(la, axis=0, keepdims=True)
        qd = q * jnp.exp(bc)
        ki = k * jnp.exp(-bc)
        ke = k * jnp.exp(bt - bc)
        dec_j = jnp.exp(bt)
        for h in range(GLA_H):
            ks = slice(GLA_DK * h, GLA_DK * (h + 1))
            qh.append(qd[:, ks].astype(BF16))
            kih.append(ki[:, ks].astype(BF16))
            keh.append(ke[:, ks].astype(BF16))
            vh.append(v[:, GLA_DV * h:GLA_DV * (h + 1)].astype(BF16))
            dec.append(dec_j[:, ks])
        yield
    att = [_dg(a, b, _NT) * tri for a, b in zip(qh, kih)]
    yield
    o_intra = [_mm(a, b) for a, b in zip(att, vh)]
    kv = [_dg(a, b, _TN) for a, b in zip(vh, keh)]
    yield
    st = [s_ref[h] for h in range(GLA_H)]
    s_prev = []
    for j in range(NSUB):
        for h in range(GLA_H):
            n = j * GLA_H + h
            s_prev.append(st[h].astype(BF16))
            st[h] = st[h] * dec[n] + kv[n]
    for h in range(GLA_H):
        s_ref[h] = st[h]
    for j in range(NSUB):
        for h in range(GLA_H):
            n = j * GLA_H + h
            o_ref[0, 0, rows_of[j], GLA_DV * h:GLA_DV * (h + 1)] = o_intra[n] + _dg(qh[n], s_prev[n], _NT)


def _shifted(x, prev_row, next_row):
    n = x.shape[0]
    row = lax.broadcasted_iota(jnp.int32, x.shape, 0)
    xm1 = jnp.where(row == 0, prev_row, pltpu.roll(x, 1, 0))
    xp1 = jnp.where(row == n - 1, next_row, pltpu.roll(x, n - 1, 0))
    return xm1, xp1


def _halo_specs(width, blk_fn):
    per = BLK // 8

    def prev_map(*idx):
        b, blk = blk_fn(*idx)
        return (b, jnp.maximum(blk * per - 1, 0), 0)

    def next_map(nrow8):
        def f(*idx):
            b, blk = blk_fn(*idx)
            return (b, jnp.minimum((blk + 1) * per, nrow8 - 1), 0)
        return f

    return (lambda: pl.BlockSpec((1, 8, width), prev_map),
            lambda nrow8: pl.BlockSpec((1, 8, width), next_map(nrow8)))


def _stream_edges(blk, nbc, nb):
    has_prev = jnp.logical_and(blk != 0, blk != nbc)
    has_next = jnp.logical_and(blk != nbc - 1, blk != nb - 1)
    return has_prev.astype(F32), has_next.astype(F32)


def _ssd_stages(u_ref, up_ref, un_ref, cw_ref, cb_ref, dtb_ref, aneg_ref, exp_ref, tri_ref,
                y_ref, xs_ref, h_ref, xdt_s, bm_s, cm_s, da_s, *, nbc, nb):
    d = pl.program_id(1)
    i = pl.program_id(2)
    blk = _block_of_step(d, i, nbc, nb)

    @pl.when(i == 0)
    def _():
        h_ref[...] = jnp.zeros_like(h_ref)

    has_prev, has_next = _stream_edges(blk, nbc, nb)
    u = u_ref[0]
    xbc = u[:, 256:768]
    prev_row = up_ref[0, 7:8, 256:768] * has_prev
    next_row = un_ref[0, 0:1, 256:768] * has_next
    xm1, xp1 = _shifted(xbc, prev_row, next_row)
    cw = cw_ref[...]
    act = _silu(cw[0:1] * xm1 + cw[1:2] * xbc + cw[2:3] * xp1 + cb_ref[...])
    xs = act[:, 0:256]
    xs_ref[0, 0] = xs
    dt = _softplus(_mm_mask_rhs(u[:, 768:896], exp_ref[...].astype(BF16)) + dtb_ref[0])
    xdt_s[...] = xs * dt
    da_s[...] = dt * aneg_ref[0]
    bm_s[...] = act[:, 256:384]
    cm_s[...] = act[:, 384:512]
    yield

    tri = tri_ref[d]
    tri_b = tri.astype(BF16)

    per_g = SSM_H // SSM_G
    per_l = LANE // SSM_P
    rows_of, cgs, bgs, bgt, xdh, xh, cum_col, cum_row, dec = [], [], [], [], [], [], [], [], []
    for j in range(NSUB):
        jj = jnp.where(d == 0, j, NSUB - 1 - j)
        rows = pl.ds(pl.multiple_of(jj * CHUNK, CHUNK), CHUNK)
        rows_of.append(rows)
        da = da_s[rows, :]
        xdt = xdt_s[rows, :]
        bm = bm_s[rows, :]
        cm = cm_s[rows, :]
        cum = _mm_mask_lhs(tri_b, da)
        tot = jnp.sum(da, axis=0, keepdims=True)
        cum_t = [cum[:, LANE * c:LANE * (c + 1)].T for c in range(HEAD_W // LANE)]
        bm_t = bm.T.astype(BF16)
        for h in range(SSM_H):
            hs = slice(SSM_P * h, SSM_P * (h + 1))
            gs = slice(SSM_N * (h // per_g), SSM_N * (h // per_g + 1))
            cgs.append(cm[:, gs])
            bgs.append(bm[:, gs])
            bgt.append(bm_t[gs, :])
            xdh.append(xdt[:, hs])
            cum_row.append(cum_t[h // per_l][SSM_P * (h % per_l):SSM_P * (h % per_l) + 1, :])
            xh.append(xdt[:, hs].astype(BF16))
            cum_col.append(cum[:, hs])
            dec.append(tot[:, hs])
        yield
    cb = [_dg(c.astype(BF16), b.astype(BF16), _NT) for c, b in zip(cgs[::per_g], bgs[::per_g])]
    seg = [jnp.where(tri > 0, jnp.exp(c - r), 0.0) for c, r in zip(cum_col, cum_row)]
    yield
    y_intra = [_mm(cb[n // per_g] * s, x) for n, (s, x) in enumerate(zip(seg, xh))]
    contrib = [_dg(b, (x * jnp.exp(t - c)).astype(BF16), _NN) for b, t, c, x in zip(bgt, dec, cum_col, xdh)]
    yield
    hp = [h_ref[h] for h in range(SSM_H)]
    h_prev = []
    for j in range(NSUB):
        for h in range(SSM_H):
            n = j * SSM_H + h
            h_prev.append(hp[h])
            hp[h] = hp[h] * jnp.exp(dec[n]) + contrib[n]
    for h in range(SSM_H):
        h_ref[h] = hp[h]
    for j in range(NSUB):
        for h in range(SSM_H):
            n = j * SSM_H + h
            y_ref[0, 0, rows_of[j], SSM_P * h:SSM_P * (h + 1)] = (
                y_intra[n] + _mm(cgs[n] * jnp.exp(cum_col[n]), h_prev[n]))


def _rwkv_prep(u, prev_row, next_row, mu, a0, wa, wg, k_k, k_a, bd, r_ref, k_ref, v_ref, g_ref, kkn_ref, ab_ref,
               tlw_ref):
    xm1, xp1 = _shifted(u, prev_row, next_row)
    u = u + mu[0:1] * (xm1 - u) + mu[1:2] * (xp1 - u)
    r = u[:, 0:256]
    k = u[:, 256:512]
    lwa = u[:, 768:896]
    a = _sigmoid(a0 + _mm(lwa, wa))
    g = _mm(_sigmoid(u[:, 896:1024]), wg)
    kk = k * k_k
    ss = _seg_sum(kk * kk, bd)
    kk = kk / jnp.maximum(jnp.sqrt(ss), 1e-12)
    r_ref[0] = r
    k_ref[0] = k * (1.0 + (a - 1.0) * k_a)
    v_ref[0] = u[:, 512:768]
    g_ref[0] = g
    kkn_ref[0] = kk
    ab_ref[0] = kk * a
    tlw_ref[0] = jnp.tanh(lwa)


def _rwkv_stages(r_ref, k_ref, v_ref, kk_ref, ab_ref, tlw_ref, w0_ref, wdec_ref, tri_ref, stri_ref,
                 bd16_ref, eye_ref, y_ref, h_ref):
    d = pl.program_id(1)
    i = pl.program_id(2)

    @pl.when(i == 0)
    def _():
        h_ref[...] = jnp.zeros_like(h_ref)

    incl2 = tri_ref[d]
    strict2 = stri_ref[d]
    incl = incl2[:, :CHUNK]
    strict = strict2[:, :CHUNK]
    incl_b = incl.astype(BF16)
    bd16 = bd16_ref[...]
    eye = eye_ref[...]
    w0 = w0_ref[0]
    wdec = wdec_ref[0].astype(BF16)

    mask2 = jnp.concatenate([strict, incl], axis=0)
    stack = lambda a, b: jnp.concatenate([a, b], axis=0)
    heads = [slice(RWKV_N * h, RWKV_N * (h + 1)) for h in range(RWKV_H)]
    bf = lambda x: x.astype(BF16)

    rows_of, ar, rt, bt, kt, vh, bh, kh, pc = [], [], [], [], [], [], [], [], []
    for j in range(NSUB):
        jj = jnp.where(d == 0, j, NSUB - 1 - j)
        rows = pl.ds(pl.multiple_of(jj * CHUNK, CHUNK), CHUNK)
        rows_of.append(rows)
        wr = w0 + _mm(tlw_ref[0, rows, :], wdec)
        lw = -jnp.exp(-_softplus(-wr) - 0.5)
        gc = _mm_mask_lhs(incl_b, lw)
        tot = jnp.sum(lw, axis=0, keepdims=True)
        eng = jnp.exp(-gc)
        e_end = jnp.exp(tot - gc)
        abv = ab_ref[0, rows, :]
        k2 = k_ref[0, rows, :]
        at_j = bf(-kk_ref[0, rows, :] * jnp.exp(gc - lw))
        rt_f = r_ref[0, rows, :] * jnp.exp(gc)
        rt_j = bf(rt_f)
        bt_j = bf(abv * eng)
        kt_j = bf(k2 * eng)
        v_j = bf(v_ref[0, rows, :])
        bh_j = bf(abv * e_end)
        kh_j = bf(k2 * e_end)
        pc_j = jnp.exp(tot)
        for hs in heads:
            ar.append(stack(at_j[:, hs], rt_j[:, hs]))
            rt.append(rt_f[:, hs])
            bt.append(stack(bt_j[:, hs], bt_j[:, hs]))
            kt.append(kt_j[:, hs])
            vh.append(v_j[:, hs])
            bh.append(bh_j[:, hs])
            kh.append(kh_j[:, hs])
            pc.append(pc_j[:, hs])
        yield

    g1 = [_dg(a, b, _NT) for a, b in zip(ar, bt)]
    g2 = [_dg(a, b, _NT) for a, b in zip(ar, kt)]
    yield
    a_ab = [g[:CHUNK] * strict2 for g in g1]
    a_rb = [bf(g[CHUNK:, :CHUNK] * incl) for g in g1]
    avr = [_dg(bf(g * mask2), v, _NN) for g, v in zip(g2, vh)]
    khv = [_dg(v, k, _TN) for v, k in zip(vh, kh)]
    yield
    ad = [a * bd16 for a in a_ab]
    ee = [_parts(a - b) for a, b in zip(a_ab, ad)]
    q = [eye + a for a in ad]
    pw = [_kdot(x, x) for x in map(_parts, ad)]
    yield
    for _ in range(2):
        res = [_kdot(_parts(stack(a, x)), _parts(a)) for a, x in zip(pw, q)]
        pw = [r[:CHUNK] for r in res]
        q = [x + r[CHUNK:] for x, r in zip(q, res)]
        yield
    p = [x + _kdot(_parts(x), _parts(a)) for x, a in zip(q, pw)]
    yield
    p_s = [_parts(x) for x in p]
    f = [_parts(_kdot(x, e)) for x, e in zip(p_s, ee)]
    yield
    f2 = [_parts(_kdot(x, x)) for x in f]
    t1 = [x + _kdot(a, xs) for x, a, xs in zip(p, f, p_s)]
    yield
    tinv = [bf((x + _kdot(a, _parts(x)))[:, :CHUNK]) for x, a in zip(t1, f2)]
    z = [jnp.concatenate([a[:CHUNK], bf(c[:CHUNK])], axis=1) for a, c in zip(ar, avr)]
    yield
    tz = [bf(_dg(t, x, _NN)) for t, x in zip(tinv, z)]
    w = [bf(_dg(t, b, _TN)) for t, b in zip(tinv, bh)]
    yield
    rz = [_dg(b, x, _NN) for b, x in zip(a_rb, tz)]
    mn = [_dg(x, y, _TN) for x, y in zip(z, w)]
    qm = [bf(r + x[:, :CHUNK]) for r, x in zip(rt, rz)]
    y0 = [c[CHUNK:] + x[:, CHUNK:] for c, x in zip(avr, rz)]
    m1 = [bf(x[:CHUNK]) for x in mn]
    n1 = [x[CHUNK:] + e for x, e in zip(mn, khv)]
    yield

    ht = [h_ref[h] for h in range(RWKV_H)]
    for j in range(NSUB):
        for h in range(RWKV_H):
            n = j * RWKV_H + h
            hb = bf(ht[h])
            y_ref[0, 0, rows_of[j], heads[h]] = _dg(qm[n], hb, _NT) + y0[n]
            ht[h] = ht[h] * pc[n] + _dg(hb, m1[n], _NN) + n1[n]
        yield
    for h in range(RWKV_H):
        h_ref[h] = ht[h]


def _interleave(main, side):
    k = 0
    for _ in main:
        for _ in range(len(side)):
            gen = side[k % len(side)]
            k += 1
            if next(gen, StopIteration) is not StopIteration:
                break
    for gen in side:
        for _ in gen:
            pass


def _scan_kernel(ug_ref, gwd_ref, gbd_ref, us_ref, up_ref, un_ref, cw_ref, cb_ref, dtb_ref, aneg_ref, exp_ref,
                 r_ref, k_ref, v_ref, kk_ref, ab_ref, tlw_ref, w0_ref, wdec_ref,
                 tri_ref, tri2_ref, stri2_ref, bd16_ref, eye_ref,
                 gla_o_ref, ssd_y_ref, ssd_xs_ref, rw_y_ref,
                 gla_s, ssd_h, xdt_s, bm_s, cm_s, da_s, rw_h, *, nbc, nb):
    rwkv = _rwkv_stages(r_ref, k_ref, v_ref, kk_ref, ab_ref, tlw_ref, w0_ref, wdec_ref, tri2_ref, stri2_ref,
                        bd16_ref, eye_ref, rw_y_ref, rw_h)
    ssd = _ssd_stages(us_ref, up_ref, un_ref, cw_ref, cb_ref, dtb_ref, aneg_ref, exp_ref, tri_ref,
                      ssd_y_ref, ssd_xs_ref, ssd_h, xdt_s, bm_s, cm_s, da_s, nbc=nbc, nb=nb)
    gla = _gla_stages(ug_ref, gwd_ref, gbd_ref, tri_ref, gla_o_ref, gla_s)
    _interleave(rwkv, [ssd, gla])


def _scans(u_gla, gla_wd, gla_bd, u_ssm, conv_w, conv_b, dtb, aneg, expand,
           r, k2, v, kk, ab, tlw, rw_w0, rw_wd, tri, stri, bd16, eye, nbc):
    bsz, t, _ = r.shape
    nb = t // BLK
    blk = lambda b, d, i: _block_of_step(d, i, nbc, nb)
    tok = lambda w_: pl.BlockSpec((1, BLK, w_), lambda b, d, i: (b, blk(b, d, i), 0))
    full = lambda shape: pl.BlockSpec(shape, lambda b, d, i: (0,) * len(shape))
    by_dir = lambda *shape: pl.BlockSpec((1,) + shape, lambda b, d, i: (d,) + (0,) * len(shape))
    prev_spec, next_spec = _halo_specs(SSM_W, lambda b, d, i: (b, blk(b, d, i)))
    dup = lambda m: jnp.concatenate([m, m], axis=-1)
    out = pl.BlockSpec((1, 1, BLK, HEAD_W), lambda b, d, i: (d, b, blk(b, d, i), 0))
    return pl.pallas_call(
        functools.partial(_scan_kernel, nbc=nbc, nb=nb),
        grid=(bsz, 2, nb),
        in_specs=[tok(GLA_W), by_dir(LANE, LANE), by_dir(1, LANE),
                  tok(SSM_W), prev_spec(), next_spec(t // 8), full((3, 512)), full((1, 512)),
                  by_dir(1, HEAD_W), by_dir(1, HEAD_W), full((LANE, HEAD_W))]
        + [tok(HEAD_W)] * 5 + [tok(LANE), by_dir(1, HEAD_W), by_dir(LANE, HEAD_W),
                               full((2, CHUNK, CHUNK)), full((2, CHUNK, LANE)), full((2, CHUNK, LANE)),
                               full((CHUNK, LANE)), full((CHUNK, LANE))],
        out_specs=[out] * 4,
        out_shape=[jax.ShapeDtypeStruct((2, bsz, t, HEAD_W), F32)] * 4,
        scratch_shapes=[pltpu.VMEM((GLA_H, GLA_DV, GLA_DK), F32), pltpu.VMEM((SSM_H, SSM_N, SSM_P), F32),
                        pltpu.VMEM((BLK, HEAD_W), F32), pltpu.VMEM((BLK, LANE), F32),
                        pltpu.VMEM((BLK, LANE), F32), pltpu.VMEM((BLK, HEAD_W), F32),
                        pltpu.VMEM((RWKV_H, RWKV_N, RWKV_N), F32)],
        compiler_params=_cparams(("arbitrary", "arbitrary", "arbitrary")),
        name="scans",
    )(u_gla, gla_wd, gla_bd, u_ssm, u_ssm, u_ssm, conv_w, conv_b, dtb, aneg, expand,
      r, k2, v, kk, ab, tlw, rw_w0, rw_wd, tri, dup(tri), dup(stri), dup(bd16), dup(eye))


def _att_kernel(q_ref, k_ref, vt_ref, o_ref, *, nbc, n_ctx):
    i = pl.program_id(1)
    per_kv = ATT_HQ // ATT_HKV

    def attend(n_keys):
        qt = q_ref[0].T.astype(BF16)
        heads = range(ATT_HQ)
        s = [jnp.dot(k_ref[0, h // per_kv, :n_keys, :], qt[h * ATT_HD:(h + 1) * ATT_HD],
                     preferred_element_type=F32) for h in heads]
        e = [jnp.exp(x - jnp.max(x, axis=0, keepdims=True)) for x in s]
        l = [jnp.sum(x, axis=0, keepdims=True) for x in e]
        for h in heads:
            ot = jnp.dot(vt_ref[0, h // per_kv, :, :n_keys], e[h].astype(BF16), preferred_element_type=F32) / l[h]
            o_ref[0, :, h * ATT_HD:(h + 1) * ATT_HD] = ot.T

    @pl.when(i < nbc)
    def _():
        attend(n_ctx)

    @pl.when(i >= nbc)
    def _():
        attend(k_ref.shape[2])


def _attention(q, k, vt, nbc, n_ctx):
    bsz, t, _ = q.shape
    nb = t // BLK
    return pl.pallas_call(
        functools.partial(_att_kernel, nbc=nbc, n_ctx=n_ctx),
        grid=(bsz, nb),
        in_specs=[pl.BlockSpec((1, BLK, HEAD_W), lambda b, i: (b, i, 0)),
                  pl.BlockSpec((1, ATT_HKV, t, ATT_HD), lambda b, i: (b, 0, 0, 0)),
                  pl.BlockSpec((1, ATT_HKV, ATT_HD, t), lambda b, i: (b, 0, 0, 0))],
        out_specs=pl.BlockSpec((1, BLK, HEAD_W), lambda b, i: (b, i, 0)),
        out_shape=jax.ShapeDtypeStruct((bsz, t, HEAD_W), F32),
        compiler_params=_cparams(("arbitrary", "arbitrary")),
        name="attention",
    )(q, k, vt)


def _out_kernel(h_ref, mod_ref, glaf_ref, glab_ref, glag_ref, ssdf_ref, ssdb_ref, xs_ref, z_ref,
                rwf_ref, rwb_ref, r_ref, k_ref, v_ref, g_ref, att_ref,
                glan_ref, ssd_d_ref, ssdn_ref, rk_ref, lng_ref, lnb_ref, bd_ref, w_ref, o_ref):
    bd = bd_ref[...].astype(BF16)
    seg_mean = lambda x: _seg_sum(x, bd) * (1.0 / 64.0)
    o = glaf_ref[0, 0] + glab_ref[0, 0]
    y_gla = o * lax.rsqrt(seg_mean(o * o) + EPS) * glan_ref[...] * _silu(glag_ref[0])
    y = ssdf_ref[0, 0] + ssdb_ref[0, 0] + ssd_d_ref[...] * xs_ref[0, 0]
    y = y * _silu(z_ref[0])
    y_ssd = y * lax.rsqrt(jnp.mean(y * y, axis=-1, keepdims=True) + EPS) * ssdn_ref[...]
    y = rwf_ref[0, 0] + rwb_ref[0, 0]
    mu = seg_mean(y)
    yc = y - mu
    var = seg_mean(yc * yc)
    yn = yc * lax.rsqrt(var + RWKV_GN_EPS) * lng_ref[...] + lnb_ref[...]
    v = v_ref[0]
    bonus = _seg_sum(r_ref[0] * k_ref[0] * rk_ref[...], bd) * v
    y_rw = (yn + bonus) * g_ref[0]
    w = w_ref[...]
    proj = (_mm(y_gla, w[0:256]) + _mm(y_ssd, w[256:512])) + (_mm(y_rw, w[512:768]) + _mm(att_ref[0], w[768:1024]))
    o_ref[0] = h_ref[0] + mod_ref[0][5:6] * proj


def _mix_out(h, mod, gla_o, u_gla, ssd_y, ssd_xs, u_ssm, rw_y, r, k2, v, g, att_o,
             gla_n, ssd_d, ssd_n, r_k, ln_g, ln_b, bd64, w_out, layer, nbc):
    bsz, t, d = h.shape
    tok = lambda w_, col=0: pl.BlockSpec((1, BLK, w_), lambda b, i: (b, i, col))
    dirn = lambda dd: pl.BlockSpec((1, 1, BLK, HEAD_W), lambda b, i: (dd, b, i, 0))
    full = lambda shape: pl.BlockSpec(shape, lambda b, i: (0,) * len(shape))
    vec = full((1, HEAD_W))
    return pl.pallas_call(
        _out_kernel,
        grid=(bsz, t // BLK),
        in_specs=[tok(d), _mod_spec(layer, nbc, d),
                  dirn(0), dirn(1), tok(HEAD_W, 2),
                  dirn(0), dirn(1), dirn(0), tok(HEAD_W, 0),
                  dirn(0), dirn(1), tok(HEAD_W), tok(HEAD_W), tok(HEAD_W), tok(HEAD_W), tok(HEAD_W),
                  vec, vec, vec, vec, vec, vec, full((HEAD_W, HEAD_W)),
                  pl.BlockSpec((None, d, d), lambda b, i: (layer, 0, 0))],
        out_specs=tok(d),
        out_shape=jax.ShapeDtypeStruct(h.shape, F32),
        compiler_params=_cparams(("arbitrary", "arbitrary")),
        name="mix_out",
    )(h, mod, gla_o, gla_o, u_gla, ssd_y, ssd_y, ssd_xs, u_ssm, rw_y, rw_y, r, k2, v, g, att_o,
      gla_n, ssd_d, ssd_n, r_k, ln_g, ln_b, bd64, w_out)


def _rope_tables(n_ctx, n_lat):
    rows = n_lat // GRID_W
    row = jnp.repeat(jnp.arange(rows, dtype=F32), GRID_W)
    col = jnp.tile(jnp.arange(GRID_W, dtype=F32), rows)
    inv = ROPE_THETA ** (-jnp.arange(0, ROPE_AXIS_DIM, 2, dtype=F32) / ROPE_AXIS_DIM)
    ang = jnp.stack([row[:, None] * inv, col[:, None] * inv], axis=1)
    cos, sin = jnp.cos(ang), jnp.sin(ang)
    zero = jnp.zeros_like(sin)
    c = jnp.concatenate([cos, cos], axis=-1).reshape(n_lat, ATT_HD)
    sa = jnp.concatenate([-sin, zero], axis=-1).reshape(n_lat, ATT_HD)
    sb = jnp.concatenate([zero, sin], axis=-1).reshape(n_lat, ATT_HD)
    pad = lambda x, fill: jnp.concatenate([jnp.full((n_ctx, ATT_HD), fill, F32), x], axis=0)
    two = lambda x: jnp.concatenate([x, x], axis=1)
    return two(pad(c, 1.0)), two(pad(sa, 0.0)), two(pad(sb, 0.0))


def _masks():
    t = jnp.arange(CHUNK)
    lower = (t[None, :] <= t[:, None]).astype(F32)
    slower = (t[None, :] < t[:, None]).astype(F32)
    tri = jnp.stack([lower, lower.T])
    stri = jnp.stack([slower, slower.T])
    bd16 = (t[None, :] // SUB == t[:, None] // SUB).astype(F32)
    eye = jnp.eye(CHUNK, dtype=F32)
    c = jnp.arange(HEAD_W)
    bd64 = (c[None, :] // 64 == c[:, None] // 64).astype(F32)
    expand = (jnp.arange(LANE)[:, None] == c[None, :] // 64).astype(F32)
    return tri, stri, bd16, eye, bd64, expand


def _pad_cols(x, width):
    return jnp.pad(x, [(0, 0)] * (x.ndim - 1) + [(0, width - x.shape[-1])])


def _pad_rows(x, height, top=0):
    return jnp.pad(x, [(0, 0)] * (x.ndim - 2) + [(top, height - top - x.shape[-2]), (0, 0)])


def _rep(x, n):
    return jnp.repeat(x, n, axis=-1)


def kernel(x, c, ctx, c_ctx, norm_g, w_mod, b_mod, ffn_in, ffn_out, w_in, w_out, gla_w_dec, gla_b_dec, gla_norm, ssm_conv_w, ssm_conv_b, ssm_dt_bias, ssm_a_log, ssm_d, ssm_norm, rwkv_shift_mu, rwkv_w0, rwkv_w_dec, rwkv_a0, rwkv_w_a, rwkv_w_g, rwkv_k_k, rwkv_k_a, rwkv_r_k, rwkv_ln_g, rwkv_ln_b, att_q_norm, att_k_norm):
    bsz, n_lat, d = x.shape
    n_ctx = ctx.shape[1]
    depth = w_mod.shape[0]
    assert bsz == 4 and n_ctx % BLK == 0 and n_lat % BLK == 0 and n_lat % GRID_W == 0
    nbc = n_ctx // BLK
    tri, stri, bd16, eye, bd64, expand = _masks()
    rope_c, rope_sa, rope_sb = _rope_tables(n_ctx, n_lat)

    cvec = jnp.concatenate([c, c_ctx[None], jnp.zeros((8 - bsz - 1, d), F32)], axis=0)
    mod_all = _compute_mod(cvec, w_mod, b_mod).reshape(depth, 8, N_MOD, d)

    w1 = ffn_in.astype(BF16)
    w2 = ffn_out.astype(BF16)
    wo = w_out.astype(BF16)
    o_ssm = 784
    o_rw = o_ssm + 772
    o_att = o_rw + 1024
    w_proj = jnp.concatenate([_pad_cols(w_in[..., :o_ssm], GLA_W), _pad_cols(w_in[..., o_ssm:o_rw], SSM_W),
                              w_in[..., o_rw:o_att], w_in[..., o_att:]], axis=-1).astype(BF16)

    h = x
    for l in range(depth):
        gla_wd = _pad_rows(gla_w_dec[l], LANE)
        gla_bd = gla_b_dec[l][:, None, :]
        ssm_dtb = _rep(ssm_dt_bias[l], SSM_P)[:, None, :]
        ssm_an = _rep(-jnp.exp(ssm_a_log[l]), SSM_P)[:, None, :]
        rw_wd = _pad_rows(rwkv_w_dec[l], LANE)
        rw_wa = _pad_rows(rwkv_w_a[l], LANE, top=64)
        rw_w0 = rwkv_w0[l][:, None, :]

        h = _ffn(h, mod_all, norm_g[l, 0][None], w1, w2, l, 0, nbc, ctx=ctx if l == 0 else None)
        u_gla, u_ssm, qn, kt, vt, r, k2, v, g, kk, ab, tlw = _inproj(
            h, mod_all, norm_g[l, 1][None], w_proj, rope_c, rope_sa, rope_sb, jnp.tile(att_q_norm[l], ATT_HQ)[None],
            jnp.tile(att_k_norm[l], ATT_HKV)[None], bd64, rwkv_shift_mu[l], rwkv_a0[l][None], rw_wa, rwkv_w_g[l],
            rwkv_k_k[l][None], rwkv_k_a[l][None], l, nbc)
        gla_o, ssd_y, ssd_xs, rw_y = _scans(u_gla, gla_wd, gla_bd, u_ssm, ssm_conv_w[l], ssm_conv_b[l][None], ssm_dtb,
                                            ssm_an, expand, r, k2, v, kk, ab, tlw, rw_w0, rw_wd, tri, stri, bd16,
                                            eye, nbc)
        att_o = _attention(qn, kt, vt, nbc, n_ctx)

        h = _mix_out(h, mod_all, gla_o, u_gla, ssd_y, ssd_xs, u_ssm, rw_y, r, k2, v, g, att_o,
                     jnp.tile(gla_norm[l], GLA_H)[None], _rep(ssm_d[l], SSM_P)[None], ssm_norm[l][None],
                     rwkv_r_k[l].reshape(1, HEAD_W), rwkv_ln_g[l][None], rwkv_ln_b[l][None], bd64,
                     wo, l, nbc)
        h = _ffn(h, mod_all, norm_g[l, 2][None], w1, w2, l, 1, nbc, latent_only=l == depth - 1)
    return h
```

```python
import functools
import math

import jax
import jax.numpy as jnp
from jax import lax
from jax.experimental import pallas as pl
from jax.experimental.pallas import tpu as pltpu

F32 = jnp.float32
BF16 = jnp.bfloat16

N_MOD = 9
EPS = 1e-6
GLA_H, GLA_DK, GLA_DV, GLA_LORA, GLA_TAU = 4, 32, 64, 16, 16.0
SSM_H, SSM_P, SSM_G, SSM_N = 4, 64, 2, 64
RWKV_H, RWKV_N, RWKV_GN_EPS = 4, 64, 64e-5
ATT_HQ, ATT_HKV, ATT_HD = 4, 2, 64
GRID_W = 64
ROPE_THETA = 10000.0
ROPE_AXIS_DIM = ATT_HD // 2

CHUNK = 64
SUB = 16
BLK = 256
NSUB = BLK // CHUNK
LANE = 128
HEAD_W = 256
VMEM_LIMIT = 56 * 1024 * 1024

GLA_W = 896
SSM_W = 896
RWKV_W = 1024
ATT_W = 512


def _cparams(sem):
    return pltpu.CompilerParams(dimension_semantics=sem, vmem_limit_bytes=VMEM_LIMIT)


def _mm(a, b):
    return jnp.dot(a.astype(BF16), b.astype(BF16), preferred_element_type=F32)


def _dg(a, b, dims):
    return lax.dot_general(a, b, (dims, ((), ())), preferred_element_type=F32)


_NN = ((1,), (0,))
_NT = ((1,), (1,))
_TN = ((0,), (0,))


def _split2(x):
    hi = x.astype(BF16)
    lo = (x - hi.astype(F32)).astype(BF16)
    return hi, lo


def _split3(x):
    hi = x.astype(BF16)
    r1 = x - hi.astype(F32)
    mid = r1.astype(BF16)
    lo = (r1 - mid.astype(F32)).astype(BF16)
    return hi, mid, lo


_parts = _split2


def _kdot(a, b):
    ah, al = a
    bh, bl = b
    return _dg(jnp.concatenate([ah, al], axis=1), jnp.concatenate([bh, bl, bh, bl], axis=0), _NN)


def _mm_mask_lhs(mask_bf16, x, dims=_NN):
    hi, mid, lo = _split3(x)
    return _dg(mask_bf16, hi, dims) + (_dg(mask_bf16, mid, dims) + _dg(mask_bf16, lo, dims))


def _mm_mask_rhs(x, mask_bf16, dims=_NN):
    hi, mid, lo = _split3(x)
    return _dg(hi, mask_bf16, dims) + (_dg(mid, mask_bf16, dims) + _dg(lo, mask_bf16, dims))


def _seg_sum(x, mask_bf16):
    hi, lo = _split2(x)
    return _dg(hi, mask_bf16, _NN) + _dg(lo, mask_bf16, _NN)


def _sigmoid(x):
    return jax.nn.sigmoid(x)


def _silu(x):
    return x * jax.nn.sigmoid(x)


def _softplus(x):
    return jnp.maximum(x, 0.0) + jnp.log1p(jnp.exp(-jnp.abs(x)))


def _modulate(x, g, shift, scale):
    ms = jnp.mean(x * x, axis=-1, keepdims=True)
    return (x * lax.rsqrt(ms + EPS) * g) * (1.0 + scale) + shift


def _block_of_step(d, i, nbc, nb):
    back = jnp.where(i < nbc, nbc - 1 - i, nb + nbc - 1 - i)
    return jnp.where(d == 0, i, back)


def _mod_row(b, i, nbc):
    return jnp.where(i < nbc, 4, b)


def _mod_kernel(c_ref, w_ref, b_ref, o_ref):
    o_ref[0] = _mm(_silu(c_ref[...]), w_ref[0]) + b_ref[0]


def _compute_mod(cvec, w_mod, b_mod):
    depth, d, n = w_mod.shape
    tn = 1024
    return pl.pallas_call(
        _mod_kernel,
        grid=(depth, n // tn),
        in_specs=[pl.BlockSpec((8, d), lambda l, j: (0, 0)),
                  pl.BlockSpec((1, d, tn), lambda l, j: (l, 0, j)),
                  pl.BlockSpec((1, 1, tn), lambda l, j: (l, 0, j))],
        out_specs=pl.BlockSpec((1, 8, tn), lambda l, j: (l, 0, j)),
        out_shape=jax.ShapeDtypeStruct((depth, 8, n), F32),
        compiler_params=_cparams(("arbitrary", "arbitrary")),
        name="adaln_mod",
    )(cvec, w_mod, b_mod.reshape(depth, 1, n))


def _ffn_math(x, m, g, w1_ref, w2_ref, i0, d_ff):
    hm = _modulate(x, g, m[i0:i0 + 1], m[i0 + 1:i0 + 2]).astype(BF16)
    ab = jnp.dot(hm, w1_ref[...], preferred_element_type=F32)
    a = ab[:, :d_ff]
    act = (_silu(a) * ab[:, d_ff:]).astype(BF16)
    y = jnp.dot(act, w2_ref[...], preferred_element_type=F32)
    return x + (0.5 * m[i0 + 2:i0 + 3]) * y


def _ffn_kernel(h_ref, mod_ref, g_ref, w1_ref, w2_ref, o_ref, *, i0, d_ff):
    o_ref[0] = _ffn_math(h_ref[0], mod_ref[0], g_ref[...], w1_ref, w2_ref, i0, d_ff)


def _ffn_first_kernel(ctx_ref, x_ref, mod_ref, g_ref, w1_ref, w2_ref, o_ref, *, i0, d_ff, nbc):
    x = jnp.where(pl.program_id(1) < nbc, ctx_ref[0], x_ref[0])
    o_ref[0] = _ffn_math(x, mod_ref[0], g_ref[...], w1_ref, w2_ref, i0, d_ff)


def _ffn_last_kernel(h_ref, mod_ref, g_ref, w1_ref, w2_ref, o_ref, *, i0, d_ff, nbc):
    @pl.when(pl.program_id(1) >= nbc)
    def _():
        o_ref[0] = _ffn_math(h_ref[0], mod_ref[0], g_ref[...], w1_ref, w2_ref, i0, d_ff)


def _mod_spec(layer, nbc, d):
    return pl.BlockSpec((None, 1, N_MOD, d), lambda b, i: (layer, _mod_row(b, i, nbc), 0, 0))


def _ffn(h, mod, g, w1, w2, layer, which, nbc, ctx=None, latent_only=False):
    bsz, _, d = h.shape
    d_ff = w2.shape[2]
    t = h.shape[1] + (0 if ctx is None else ctx.shape[1])
    tok = pl.BlockSpec((1, BLK, d), lambda b, i: (b, i, 0))
    lat = pl.BlockSpec((1, BLK, d), lambda b, i: (b, jnp.maximum(i - nbc, 0), 0))
    common = [_mod_spec(layer, nbc, d),
              pl.BlockSpec((1, d), lambda b, i: (0, 0)),
              pl.BlockSpec((None, None, d, 2 * d_ff), lambda b, i: (layer, which, 0, 0)),
              pl.BlockSpec((None, None, d_ff, d), lambda b, i: (layer, which, 0, 0))]
    if ctx is not None:
        body = functools.partial(_ffn_first_kernel, i0=6 * which, d_ff=d_ff, nbc=nbc)
        ins = [pl.BlockSpec((1, BLK, d), lambda b, i: (b, jnp.minimum(i, nbc - 1), 0)), lat]
        args = (ctx, h)
    else:
        body = functools.partial(_ffn_last_kernel if latent_only else _ffn_kernel, i0=6 * which, d_ff=d_ff,
                                 **({"nbc": nbc} if latent_only else {}))
        ins = [tok]
        args = (h,)
    return pl.pallas_call(
        body,
        grid=(bsz, t // BLK),
        in_specs=ins + common,
        out_specs=lat if latent_only else tok,
        out_shape=jax.ShapeDtypeStruct((bsz, t - nbc * BLK if latent_only else t, d), F32),
        compiler_params=_cparams(("arbitrary", "arbitrary")),
        name="ffn",
    )(*args, mod, g, w1, w2)


def _rope(x, c, sa, sb):
    w = x.shape[1]
    return x * c + pltpu.roll(x, w - ROPE_AXIS_DIM // 2, 1) * sa + pltpu.roll(x, ROPE_AXIS_DIM // 2, 1) * sb


def _att_prep(u, c, sa, sb, qg, kg, bd, q_ref, k_ref, vt_ref):
    c2 = jnp.concatenate([c, c], axis=1)
    sa2 = jnp.concatenate([sa, sa], axis=1)
    sb2 = jnp.concatenate([sb, sb], axis=1)
    q = u[:, 0:256]
    k = u[:, 256:384]
    qms = _seg_sum(q * q, bd) * (1.0 / ATT_HD)
    kms = _seg_sum(k * k, bd[:LANE, :LANE]) * (1.0 / ATT_HD)
    qn = q * lax.rsqrt(qms + EPS) * qg
    kn = k * lax.rsqrt(kms + EPS) * kg
    q_ref[0] = _rope(qn, c2, sa2, sb2) * (ATT_HD ** -0.5)
    kr = _rope(kn, c, sa, sb).astype(BF16)
    k_ref[0, 0] = kr[:, :ATT_HD]
    k_ref[0, 1] = kr[:, ATT_HD:]
    vt = u[:, 384:512].T.astype(BF16)
    vt_ref[0, 0] = vt[:ATT_HD]
    vt_ref[0, 1] = vt[ATT_HD:]


def _inproj_kernel(h_ref, hp_ref, hn_ref, mod_ref, g_ref, w_ref, c_ref, sa_ref, sb_ref, qg_ref, kg_ref, bd_ref,
                   mu_ref, a0_ref, wa_ref, wg_ref, kk_ref, ka_ref,
                   gla_ref, ssm_ref, q_ref, k_ref, vt_ref,
                   r_ref, k2_ref, v_ref, gg_ref, kkn_ref, ab_ref, tlw_ref, *, nbc, nb):
    m = mod_ref[0]
    g = g_ref[...]
    bd = bd_ref[...].astype(BF16)
    hm = _modulate(h_ref[0], g, m[3:4], m[4:5]).astype(BF16)
    u = jnp.dot(hm, w_ref[...], preferred_element_type=F32)
    gla_ref[0] = u[:, :GLA_W]
    ssm_ref[0] = u[:, GLA_W:GLA_W + SSM_W]
    o_rw = GLA_W + SSM_W
    _att_prep(u[:, o_rw + RWKV_W:], c_ref[...], sa_ref[...], sb_ref[...], qg_ref[...], kg_ref[...], bd,
              q_ref, k_ref, vt_ref)
    halo = jnp.concatenate([hp_ref[0], hn_ref[0]], axis=0)
    uh = jnp.dot(_modulate(halo, g, m[3:4], m[4:5]).astype(BF16), w_ref[:, o_rw:o_rw + RWKV_W],
                 preferred_element_type=F32)
    has_prev, has_next = _stream_edges(pl.program_id(1), nbc, nb)
    _rwkv_prep(u[:, o_rw:o_rw + RWKV_W], uh[7:8] * has_prev, uh[8:9] * has_next, mu_ref[...], a0_ref[...],
               wa_ref[...], wg_ref[...], kk_ref[...], ka_ref[...], bd,
               r_ref, k2_ref, v_ref, gg_ref, kkn_ref, ab_ref, tlw_ref)


def _inproj(h, mod, g, w, c, sa, sb, qg, kg, bd64, mu, a0, wa, wg, k_k, k_a, layer, nbc):
    bsz, t, d = h.shape
    nb = t // BLK
    widths = (GLA_W, SSM_W)
    full = lambda shape: pl.BlockSpec(shape, lambda b, i: (0,) * len(shape))
    tab = pl.BlockSpec((BLK, LANE), lambda b, i: (i, 0))
    tok = lambda w_: pl.BlockSpec((1, BLK, w_), lambda b, i: (b, i, 0))
    prev_spec, next_spec = _halo_specs(d, lambda b, i: (b, i))
    return pl.pallas_call(
        functools.partial(_inproj_kernel, nbc=nbc, nb=nb),
        grid=(bsz, nb),
        in_specs=[tok(d), prev_spec(), next_spec(t // 8),
                  _mod_spec(layer, nbc, d),
                  pl.BlockSpec((1, d), lambda b, i: (0, 0)),
                  pl.BlockSpec((None, d, sum(widths) + RWKV_W + ATT_W), lambda b, i: (layer, 0, 0)),
                  tab, tab, tab, full((1, HEAD_W)), full((1, LANE)), full((HEAD_W, HEAD_W)),
                  full((2, RWKV_W)), full((1, HEAD_W)), full((LANE, HEAD_W)), full((LANE, HEAD_W)),
                  full((1, HEAD_W)), full((1, HEAD_W))],
        out_specs=[tok(w_) for w_ in widths]
        + [tok(HEAD_W),
           pl.BlockSpec((1, ATT_HKV, BLK, ATT_HD), lambda b, i: (b, 0, i, 0)),
           pl.BlockSpec((1, ATT_HKV, ATT_HD, BLK), lambda b, i: (b, 0, 0, i))]
        + [tok(HEAD_W)] * 6 + [tok(LANE)],
        out_shape=[jax.ShapeDtypeStruct((bsz, t, w_), F32) for w_ in widths]
        + [jax.ShapeDtypeStruct((bsz, t, HEAD_W), F32),
           jax.ShapeDtypeStruct((bsz, ATT_HKV, t, ATT_HD), BF16),
           jax.ShapeDtypeStruct((bsz, ATT_HKV, ATT_HD, t), BF16)]
        + [jax.ShapeDtypeStruct((bsz, t, HEAD_W), F32)] * 6 + [jax.ShapeDtypeStruct((bsz, t, LANE), F32)],
        compiler_params=_cparams(("arbitrary", "arbitrary")),
        name="inproj",
    )(h, h, h, mod, g, w, c, sa, sb, qg, kg, bd64, mu, a0, wa, wg, k_k, k_a)


def _gla_stages(u_ref, wdec_ref, bdec_ref, tri_ref, o_ref, s_ref):
    d = pl.program_id(1)
    i = pl.program_id(2)

    @pl.when(i == 0)
    def _():
        s_ref[...] = jnp.zeros_like(s_ref)

    tri = tri_ref[d]
    tri_b = tri.astype(BF16)
    wdec = wdec_ref[0].astype(BF16)
    bdec = bdec_ref[0]

    rows_of, qh, kih, keh, vh, dec = [], [], [], [], [], []
    for j in range(NSUB):
        jj = jnp.where(d == 0, j, NSUB - 1 - j)
        rows = pl.ds(pl.multiple_of(jj * CHUNK, CHUNK), CHUNK)
        rows_of.append(rows)
        u = u_ref[0, rows, :]
        q = u[:, 0:128] * (GLA_DK ** -0.5)
        k = u[:, 128:256]
        v = u[:, 256:512]
        z = _mm(u[:, 768:896], wdec) + bdec
        la = -_softplus(-z) * (1.0 / GLA_TAU)
        bc = _mm_mask_lhs(tri_b, la)
        bt = jnp.sum(la, axis=0, keepdims=True)
        qd = q * jnp.exp(bc)
        ki = k * jnp.exp(-bc)
        ke = k * jnp.exp(bt - bc)
        dec_j = jnp.exp(bt)
        for h in range(GLA_H):
            ks = slice(GLA_DK * h, GLA_DK * (h + 1))
            qh.append(qd[:, ks].astype(BF16))
            kih.append(ki[:, ks].astype(BF16))
            keh.append(ke[:, ks].astype(BF16))
            vh.append(v[:, GLA_DV * h:GLA_DV * (h + 1)].astype(BF16))
            dec.append(dec_j[:, ks])
        yield
    att = [_dg(a, b, _NT) * tri for a, b in zip(qh, kih)]
    yield
    o_intra = [_mm(a, b) for a, b in zip(att, vh)]
    kv = [_dg(a, b, _TN) for a, b in zip(vh, keh)]
    yield
    st = [s_ref[h] for h in range(GLA_H)]
    s_prev = []
    for j in range(NSUB):
        for h in range(GLA_H):
            n = j * GLA_H + h
            s_prev.append(st[h].astype(BF16))
            st[h] = st[h] * dec[n] + kv[n]
    for h in range(GLA_H):
        s_ref[h] = st[h]
    for j in range(NSUB):
        for h in range(GLA_H):
            n = j * GLA_H + h
            o_ref[0, 0, rows_of[j], GLA_DV * h:GLA_DV * (h + 1)] = o_intra[n] + _dg(qh[n], s_prev[n], _NT)


def _shifted(x, prev_row, next_row):
    n = x.shape[0]
    row = lax.broadcasted_iota(jnp.int32, x.shape, 0)
    xm1 = jnp.where(row == 0, prev_row, pltpu.roll(x, 1, 0))
    xp1 = jnp.where(row == n - 1, next_row, pltpu.roll(x, n - 1, 0))
    return xm1, xp1


def _halo_specs(width, blk_fn):
    per = BLK // 8

    def prev_map(*idx):
        b, blk = blk_fn(*idx)
        return (b, jnp.maximum(blk * per - 1, 0), 0)

    def next_map(nrow8):
        def f(*idx):
            b, blk = blk_fn(*idx)
            return (b, jnp.minimum((blk + 1) * per, nrow8 - 1), 0)
        return f

    return (lambda: pl.BlockSpec((1, 8, width), prev_map),
            lambda nrow8: pl.BlockSpec((1, 8, width), next_map(nrow8)))


def _stream_edges(blk, nbc, nb):
    has_prev = jnp.logical_and(blk != 0, blk != nbc)
    has_next = jnp.logical_and(blk != nbc - 1, blk != nb - 1)
    return has_prev.astype(F32), has_next.astype(F32)


def _ssd_stages(u_ref, up_ref, un_ref, cw_ref, cb_ref, dtb_ref, aneg_ref, exp_ref, tri_ref,
                y_ref, xs_ref, h_ref, xdt_s, bm_s, cm_s, da_s, *, nbc, nb):
    d = pl.program_id(1)
    i = pl.program_id(2)
    blk = _block_of_step(d, i, nbc, nb)

    @pl.when(i == 0)
    def _():
        h_ref[...] = jnp.zeros_like(h_ref)

    has_prev, has_next = _stream_edges(blk, nbc, nb)
    u = u_ref[0]
    xbc = u[:, 256:768]
    prev_row = up_ref[0, 7:8, 256:768] * has_prev
    next_row = un_ref[0, 0:1, 256:768] * has_next
    xm1, xp1 = _shifted(xbc, prev_row, next_row)
    cw = cw_ref[...]
    act = _silu(cw[0:1] * xm1 + cw[1:2] * xbc + cw[2:3] * xp1 + cb_ref[...])
    xs = act[:, 0:256]
    xs_ref[0, 0] = xs
    dt = _softplus(_mm_mask_rhs(u[:, 768:896], exp_ref[...].astype(BF16)) + dtb_ref[0])
    xdt_s[...] = xs * dt
    da_s[...] = dt * aneg_ref[0]
    bm_s[...] = act[:, 256:384]
    cm_s[...] = act[:, 384:512]
    yield

    tri = tri_ref[d]
    tri_b = tri.astype(BF16)

    per_g = SSM_H // SSM_G
    per_l = LANE // SSM_P
    rows_of, cgs, bgs, bgt, xdh, xh, cum_col, cum_row, dec = [], [], [], [], [], [], [], [], []
    for j in range(NSUB):
        jj = jnp.where(d == 0, j, NSUB - 1 - j)
        rows = pl.ds(pl.multiple_of(jj * CHUNK, CHUNK), CHUNK)
        rows_of.append(rows)
        da = da_s[rows, :]
        xdt = xdt_s[rows, :]
        bm = bm_s[rows, :]
        cm = cm_s[rows, :]
        cum = _mm_mask_lhs(tri_b, da)
        tot = jnp.sum(da, axis=0, keepdims=True)
        cum_t = [cum[:, LANE * c:LANE * (c + 1)].T for c in range(HEAD_W // LANE)]
        bm_t = bm.T.astype(BF16)
        for h in range(SSM_H):
            hs = slice(SSM_P * h, SSM_P * (h + 1))
            gs = slice(SSM_N * (h // per_g), SSM_N * (h // per_g + 1))
            cgs.append(cm[:, gs])
            bgs.append(bm[:, gs])
            bgt.append(bm_t[gs, :])
            xdh.append(xdt[:, hs])
            cum_row.append(cum_t[h // per_l][SSM_P * (h % per_l):SSM_P * (h % per_l) + 1, :])
            xh.append(xdt[:, hs].astype(BF16))
            cum_col.append(cum[:, hs])
            dec.append(tot[:, hs])
        yield
    cb = [_dg(c.astype(BF16), b.astype(BF16), _NT) for c, b in zip(cgs[::per_g], bgs[::per_g])]
    seg = [jnp.where(tri > 0, jnp.exp(c - r), 0.0) for c, r in zip(cum_col, cum_row)]
    yield
    y_intra = [_mm(cb[n // per_g] * s, x) for n, (s, x) in enumerate(zip(seg, xh))]
    contrib = [_dg(b, (x * jnp.exp(t - c)).astype(BF16), _NN) for b, t, c, x in zip(bgt, dec, cum_col, xdh)]
    yield
    hp = [h_ref[h] for h in range(SSM_H)]
    h_prev = []
    for j in range(NSUB):
        for h in range(SSM_H):
            n = j * SSM_H + h
            h_prev.append(hp[h])
            hp[h] = hp[h] * jnp.exp(dec[n]) + contrib[n]
    for h in range(SSM_H):
        h_ref[h] = hp[h]
    for j in range(NSUB):
        for h in range(SSM_H):
            n = j * SSM_H + h
            y_ref[0, 0, rows_of[j], SSM_P * h:SSM_P * (h + 1)] = (
                y_intra[n] + _mm(cgs[n] * jnp.exp(cum_col[n]), h_prev[n]))


def _rwkv_prep(u, prev_row, next_row, mu, a0, wa, wg, k_k, k_a, bd, r_ref, k_ref, v_ref, g_ref, kkn_ref, ab_ref,
               tlw_ref):
    xm1, xp1 = _shifted(u, prev_row, next_row)
    u = u + mu[0:1] * (xm1 - u) + mu[1:2] * (xp1 - u)
    r = u[:, 0:256]
    k = u[:, 256:512]
    lwa = u[:, 768:896]
    a = _sigmoid(a0 + _mm(lwa, wa))
    g = _mm(_sigmoid(u[:, 896:1024]), wg)
    kk = k * k_k
    ss = _seg_sum(kk * kk, bd)
    kk = kk / jnp.maximum(jnp.sqrt(ss), 1e-12)
    r_ref[0] = r
    k_ref[0] = k * (1.0 + (a - 1.0) * k_a)
    v_ref[0] = u[:, 512:768]
    g_ref[0] = g
    kkn_ref[0] = kk
    ab_ref[0] = kk * a
    tlw_ref[0] = jnp.tanh(lwa)


def _rwkv_stages(r_ref, k_ref, v_ref, kk_ref, ab_ref, tlw_ref, w0_ref, wdec_ref, tri_ref, stri_ref,
                 bd16_ref, eye_ref, y_ref, h_ref):
    d = pl.program_id(1)
    i = pl.program_id(2)

    @pl.when(i == 0)
    def _():
        h_ref[...] = jnp.zeros_like(h_ref)

    incl2 = tri_ref[d]
    strict2 = stri_ref[d]
    incl = incl2[:, :CHUNK]
    strict = strict2[:, :CHUNK]
    incl_b = incl.astype(BF16)
    bd16 = bd16_ref[...]
    eye = eye_ref[...]
    w0 = w0_ref[0]
    wdec = wdec_ref[0].astype(BF16)

    mask2 = jnp.concatenate([strict, incl], axis=0)
    stack = lambda a, b: jnp.concatenate([a, b], axis=0)
    heads = [slice(RWKV_N * h, RWKV_N * (h + 1)) for h in range(RWKV_H)]
    bf = lambda x: x.astype(BF16)

    rows_of, ar, rt, bt, kt, vh, bh, kh, pc = [], [], [], [], [], [], [], [], []
    for j in range(NSUB):
        jj = jnp.where(d == 0, j, NSUB - 1 - j)
        rows = pl.ds(pl.multiple_of(jj * CHUNK, CHUNK), CHUNK)
        rows_of.append(rows)
        wr = w0 + _mm(tlw_ref[0, rows, :], wdec)
        lw = -jnp.exp(-_softplus(-wr) - 0.5)
        gc = _mm_mask_lhs(incl_b, lw)
        tot = jnp.sum(lw, axis=0, keepdims=True)
        eng = jnp.exp(-gc)
        e_end = jnp.exp(tot - gc)
        abv = ab_ref[0, rows, :]
        k2 = k_ref[0, rows, :]
        at_j = bf(-kk_ref[0, rows, :] * jnp.exp(gc - lw))
        rt_f = r_ref[0, rows, :] * jnp.exp(gc)
        rt_j = bf(rt_f)
        bt_j = bf(abv * eng)
        kt_j = bf(k2 * eng)
        v_j = bf(v_ref[0, rows, :])
        bh_j = bf(abv * e_end)
        kh_j = bf(k2 * e_end)
        pc_j = jnp.exp(tot)
        for hs in heads:
            ar.append(stack(at_j[:, hs], rt_j[:, hs]))
            rt.append(rt_f[:, hs])
            bt.append(stack(bt_j[:, hs], bt_j[:, hs]))
            kt.append(kt_j[:, hs])
            vh.append(v_j[:, hs])
            bh.append(bh_j[:, hs])
            kh.append(kh_j[:, hs])
            pc.append(pc_j[:, hs])
        yield

    g1 = [_dg(a, b, _NT) for a, b in zip(ar, bt)]
    g2 = [_dg(a, b, _NT) for a, b in zip(ar, kt)]
    yield
    a_ab = [g[:CHUNK] * strict2 for g in g1]
    a_rb = [bf(g[CHUNK:, :CHUNK] * incl) for g in g1]
    avr = [_dg(bf(g * mask2), v, _NN) for g, v in zip(g2, vh)]
    khv = [_dg(v, k, _TN) for v, k in zip(vh, kh)]
    yield
    ad = [a * bd16 for a in a_ab]
    ee = [_parts(a - b) for a, b in zip(a_ab, ad)]
    q = [eye + a for a in ad]
    pw = [_kdot(x, x) for x in map(_parts, ad)]
    yield
    for _ in range(2):
        res = [_kdot(_parts(stack(a, x)), _parts(a)) for a, x in zip(pw, q)]
        pw = [r[:CHUNK] for r in res]
        q = [x + r[CHUNK:] for x, r in zip(q, res)]
        yield
    p = [x + _kdot(_parts(x), _parts(a)) for x, a in zip(q, pw)]
    yield
    p_s = [_parts(x) for x in p]
    f = [_parts(_kdot(x, e)) for x, e in zip(p_s, ee)]
    yield
    f2 = [_parts(_kdot(x, x)) for x in f]
    t1 = [x + _kdot(a, xs) for x, a, xs in zip(p, f, p_s)]
    yield
    tinv = [bf((x + _kdot(a, _parts(x)))[:, :CHUNK]) for x, a in zip(t1, f2)]
    z = [jnp.concatenate([a[:CHUNK], bf(c[:CHUNK])], axis=1) for a, c in zip(ar, avr)]
    yield
    tz = [bf(_dg(t, x, _NN)) for t, x in zip(tinv, z)]
    w = [bf(_dg(t, b, _TN)) for t, b in zip(tinv, bh)]
    yield
    rz = [_dg(b, x, _NN) for b, x in zip(a_rb, tz)]
    mn = [_dg(x, y, _TN) for x, y in zip(z, w)]
    qm = [bf(r + x[:, :CHUNK]) for r, x in zip(rt, rz)]
    y0 = [c[CHUNK:] + x[:, CHUNK:] for c, x in zip(avr, rz)]
    m1 = [bf(x[:CHUNK]) for x in mn]
    n1 = [x[CHUNK:] + e for x, e in zip(mn, khv)]
    yield

    ht = [h_ref[h] for h in range(RWKV_H)]
    for j in range(NSUB):
        for h in range(RWKV_H):
            n = j * RWKV_H + h
            hb = bf(ht[h])
            y_ref[0, 0, rows_of[j], heads[h]] = _dg(qm[n], hb, _NT) + y0[n]
            ht[h] = ht[h] * pc[n] + _dg(hb, m1[n], _NN) + n1[n]
        yield
    for h in range(RWKV_H):
        h_ref[h] = ht[h]


def _interleave(main, side):
    for _ in main:
        for gen in side:
            next(gen, None)
    for gen in side:
        for _ in gen:
            pass


def _scan_kernel(ug_ref, gwd_ref, gbd_ref, us_ref, up_ref, un_ref, cw_ref, cb_ref, dtb_ref, aneg_ref, exp_ref,
                 r_ref, k_ref, v_ref, kk_ref, ab_ref, tlw_ref, w0_ref, wdec_ref,
                 tri_ref, tri2_ref, stri2_ref, bd16_ref, eye_ref,
                 gla_o_ref, ssd_y_ref, ssd_xs_ref, rw_y_ref,
                 gla_s, ssd_h, xdt_s, bm_s, cm_s, da_s, rw_h, *, nbc, nb):
    rwkv = _rwkv_stages(r_ref, k_ref, v_ref, kk_ref, ab_ref, tlw_ref, w0_ref, wdec_ref, tri2_ref, stri2_ref,
                        bd16_ref, eye_ref, rw_y_ref, rw_h)
    ssd = _ssd_stages(us_ref, up_ref, un_ref, cw_ref, cb_ref, dtb_ref, aneg_ref, exp_ref, tri_ref,
                      ssd_y_ref, ssd_xs_ref, ssd_h, xdt_s, bm_s, cm_s, da_s, nbc=nbc, nb=nb)
    gla = _gla_stages(ug_ref, gwd_ref, gbd_ref, tri_ref, gla_o_ref, gla_s)
    _interleave(rwkv, [gla, ssd])


def _scans(u_gla, gla_wd, gla_bd, u_ssm, conv_w, conv_b, dtb, aneg, expand,
           r, k2, v, kk, ab, tlw, rw_w0, rw_wd, tri, stri, bd16, eye, nbc):
    bsz, t, _ = r.shape
    nb = t // BLK
    blk = lambda b, d, i: _block_of_step(d, i, nbc, nb)
    tok = lambda w_: pl.BlockSpec((1, BLK, w_), lambda b, d, i: (b, blk(b, d, i), 0))
    full = lambda shape: pl.BlockSpec(shape, lambda b, d, i: (0,) * len(shape))
    by_dir = lambda *shape: pl.BlockSpec((1,) + shape, lambda b, d, i: (d,) + (0,) * len(shape))
    prev_spec, next_spec = _halo_specs(SSM_W, lambda b, d, i: (b, blk(b, d, i)))
    dup = lambda m: jnp.concatenate([m, m], axis=-1)
    out = pl.BlockSpec((1, 1, BLK, HEAD_W), lambda b, d, i: (d, b, blk(b, d, i), 0))
    return pl.pallas_call(
        functools.partial(_scan_kernel, nbc=nbc, nb=nb),
        grid=(bsz, 2, nb),
        in_specs=[tok(GLA_W), by_dir(LANE, LANE), by_dir(1, LANE),
                  tok(SSM_W), prev_spec(), next_spec(t // 8), full((3, 512)), full((1, 512)),
                  by_dir(1, HEAD_W), by_dir(1, HEAD_W), full((LANE, HEAD_W))]
        + [tok(HEAD_W)] * 5 + [tok(LANE), by_dir(1, HEAD_W), by_dir(LANE, HEAD_W),
                               full((2, CHUNK, CHUNK)), full((2, CHUNK, LANE)), full((2, CHUNK, LANE)),
                               full((CHUNK, LANE)), full((CHUNK, LANE))],
        out_specs=[out] * 4,
        out_shape=[jax.ShapeDtypeStruct((2, bsz, t, HEAD_W), F32)] * 4,
        scratch_shapes=[pltpu.VMEM((GLA_H, GLA_DV, GLA_DK), F32), pltpu.VMEM((SSM_H, SSM_N, SSM_P), F32),
                        pltpu.VMEM((BLK, HEAD_W), F32), pltpu.VMEM((BLK, LANE), F32),
                        pltpu.VMEM((BLK, LANE), F32), pltpu.VMEM((BLK, HEAD_W), F32),
                        pltpu.VMEM((RWKV_H, RWKV_N, RWKV_N), F32)],
        compiler_params=_cparams(("arbitrary", "arbitrary", "arbitrary")),
        name="scans",
    )(u_gla, gla_wd, gla_bd, u_ssm, u_ssm, u_ssm, conv_w, conv_b, dtb, aneg, expand,
      r, k2, v, kk, ab, tlw, rw_w0, rw_wd, tri, dup(tri), dup(stri), dup(bd16), dup(eye))


def _att_kernel(q_ref, k_ref, vt_ref, o_ref, *, nbc, n_ctx):
    i = pl.program_id(1)
    per_kv = ATT_HQ // ATT_HKV

    def attend(n_keys):
        qt = q_ref[0].T.astype(BF16)
        heads = range(ATT_HQ)
        s = [jnp.dot(k_ref[0, h // per_kv, :n_keys, :], qt[h * ATT_HD:(h + 1) * ATT_HD],
                     preferred_element_type=F32) for h in heads]
        e = [jnp.exp(x - jnp.max(x, axis=0, keepdims=True)) for x in s]
        l = [jnp.sum(x, axis=0, keepdims=True) for x in e]
        for h in heads:
            ot = jnp.dot(vt_ref[0, h // per_kv, :, :n_keys], e[h].astype(BF16), preferred_element_type=F32) / l[h]
            o_ref[0, :, h * ATT_HD:(h + 1) * ATT_HD] = ot.T

    @pl.when(i < nbc)
    def _():
        attend(n_ctx)

    @pl.when(i >= nbc)
    def _():
        attend(k_ref.shape[2])


def _attention(q, k, vt, nbc, n_ctx):
    bsz, t, _ = q.shape
    nb = t // BLK
    return pl.pallas_call(
        functools.partial(_att_kernel, nbc=nbc, n_ctx=n_ctx),
        grid=(bsz, nb),
        in_specs=[pl.BlockSpec((1, BLK, HEAD_W), lambda b, i: (b, i, 0)),
                  pl.BlockSpec((1, ATT_HKV, t, ATT_HD), lambda b, i: (b, 0, 0, 0)),
                  pl.BlockSpec((1, ATT_HKV, ATT_HD, t), lambda b, i: (b, 0, 0, 0))],
        out_specs=pl.BlockSpec((1, BLK, HEAD_W), lambda b, i: (b, i, 0)),
        out_shape=jax.ShapeDtypeStruct((bsz, t, HEAD_W), F32),
        compiler_params=_cparams(("arbitrary", "arbitrary")),
        name="attention",
    )(q, k, vt)


def _out_kernel(h_ref, mod_ref, glaf_ref, glab_ref, glag_ref, ssdf_ref, ssdb_ref, xs_ref, z_ref,
                rwf_ref, rwb_ref, r_ref, k_ref, v_ref, g_ref, att_ref,
                glan_ref, ssd_d_ref, ssdn_ref, rk_ref, lng_ref, lnb_ref, bd_ref, w_ref, o_ref):
    bd = bd_ref[...].astype(BF16)
    seg_mean = lambda x: _seg_sum(x, bd) * (1.0 / 64.0)
    o = glaf_ref[0, 0] + glab_ref[0, 0]
    y_gla = o * lax.rsqrt(seg_mean(o * o) + EPS) * glan_ref[...] * _silu(glag_ref[0])
    y = ssdf_ref[0, 0] + ssdb_ref[0, 0] + ssd_d_ref[...] * xs_ref[0, 0]
    y = y * _silu(z_ref[0])
    y_ssd = y * lax.rsqrt(jnp.mean(y * y, axis=-1, keepdims=True) + EPS) * ssdn_ref[...]
    y = rwf_ref[0, 0] + rwb_ref[0, 0]
    mu = seg_mean(y)
    yc = y - mu
    var = seg_mean(yc * yc)
    yn = yc * lax.rsqrt(var + RWKV_GN_EPS) * lng_ref[...] + lnb_ref[...]
    v = v_ref[0]
    bonus = _seg_sum(r_ref[0] * k_ref[0] * rk_ref[...], bd) * v
    y_rw = (yn + bonus) * g_ref[0]
    w = w_ref[...]
    proj = (_mm(y_gla, w[0:256]) + _mm(y_ssd, w[256:512])) + (_mm(y_rw, w[512:768]) + _mm(att_ref[0], w[768:1024]))
    o_ref[0] = h_ref[0] + mod_ref[0][5:6] * proj


def _mix_out(h, mod, gla_o, u_gla, ssd_y, ssd_xs, u_ssm, rw_y, r, k2, v, g, att_o,
             gla_n, ssd_d, ssd_n, r_k, ln_g, ln_b, bd64, w_out, layer, nbc):
    bsz, t, d = h.shape
    tok = lambda w_, col=0: pl.BlockSpec((1, BLK, w_), lambda b, i: (b, i, col))
    dirn = lambda dd: pl.BlockSpec((1, 1, BLK, HEAD_W), lambda b, i: (dd, b, i, 0))
    full = lambda shape: pl.BlockSpec(shape, lambda b, i: (0,) * len(shape))
    vec = full((1, HEAD_W))
    return pl.pallas_call(
        _out_kernel,
        grid=(bsz, t // BLK),
        in_specs=[tok(d), _mod_spec(layer, nbc, d),
                  dirn(0), dirn(1), tok(HEAD_W, 2),
                  dirn(0), dirn(1), dirn(0), tok(HEAD_W, 0),
                  dirn(0), dirn(1), tok(HEAD_W), tok(HEAD_W), tok(HEAD_W), tok(HEAD_W), tok(HEAD_W),
                  vec, vec, vec, vec, vec, vec, full((HEAD_W, HEAD_W)),
                  pl.BlockSpec((None, d, d), lambda b, i: (layer, 0, 0))],
        out_specs=tok(d),
        out_shape=jax.ShapeDtypeStruct(h.shape, F32),
        compiler_params=_cparams(("arbitrary", "arbitrary")),
        name="mix_out",
    )(h, mod, gla_o, gla_o, u_gla, ssd_y, ssd_y, ssd_xs, u_ssm, rw_y, rw_y, r, k2, v, g, att_o,
      gla_n, ssd_d, ssd_n, r_k, ln_g, ln_b, bd64, w_out)


def _rope_tables(n_ctx, n_lat):
    rows = n_lat // GRID_W
    row = jnp.repeat(jnp.arange(rows, dtype=F32), GRID_W)
    col = jnp.tile(jnp.arange(GRID_W, dtype=F32), rows)
    inv = ROPE_THETA ** (-jnp.arange(0, ROPE_AXIS_DIM, 2, dtype=F32) / ROPE_AXIS_DIM)
    ang = jnp.stack([row[:, None] * inv, col[:, None] * inv], axis=1)
    cos, sin = jnp.cos(ang), jnp.sin(ang)
    zero = jnp.zeros_like(sin)
    c = jnp.concatenate([cos, cos], axis=-1).reshape(n_lat, ATT_HD)
    sa = jnp.concatenate([-sin, zero], axis=-1).reshape(n_lat, ATT_HD)
    sb = jnp.concatenate([zero, sin], axis=-1).reshape(n_lat, ATT_HD)
    pad = lambda x, fill: jnp.concatenate([jnp.full((n_ctx, ATT_HD), fill, F32), x], axis=0)
    two = lambda x: jnp.concatenate([x, x], axis=1)
    return two(pad(c, 1.0)), two(pad(sa, 0.0)), two(pad(sb, 0.0))


def _masks():
    t = jnp.arange(CHUNK)
    lower = (t[None, :] <= t[:, None]).astype(F32)
    slower = (t[None, :] < t[:, None]).astype(F32)
    tri = jnp.stack([lower, lower.T])
    stri = jnp.stack([slower, slower.T])
    bd16 = (t[None, :] // SUB == t[:, None] // SUB).astype(F32)
    eye = jnp.eye(CHUNK, dtype=F32)
    c = jnp.arange(HEAD_W)
    bd64 = (c[None, :] // 64 == c[:, None] // 64).astype(F32)
    expand = (jnp.arange(LANE)[:, None] == c[None, :] // 64).astype(F32)
    return tri, stri, bd16, eye, bd64, expand


def _pad_cols(x, width):
    return jnp.pad(x, [(0, 0)] * (x.ndim - 1) + [(0, width - x.shape[-1])])


def _pad_rows(x, height, top=0):
    return jnp.pad(x, [(0, 0)] * (x.ndim - 2) + [(top, height - top - x.shape[-2]), (0, 0)])


def _rep(x, n):
    return jnp.repeat(x, n, axis=-1)


def kernel(x, c, ctx, c_ctx, norm_g, w_mod, b_mod, ffn_in, ffn_out, w_in, w_out, gla_w_dec, gla_b_dec, gla_norm, ssm_conv_w, ssm_conv_b, ssm_dt_bias, ssm_a_log, ssm_d, ssm_norm, rwkv_shift_mu, rwkv_w0, rwkv_w_dec, rwkv_a0, rwkv_w_a, rwkv_w_g, rwkv_k_k, rwkv_k_a, rwkv_r_k, rwkv_ln_g, rwkv_ln_b, att_q_norm, att_k_norm):
    bsz, n_lat, d = x.shape
    n_ctx = ctx.shape[1]
    depth = w_mod.shape[0]
    assert bsz == 4 and n_ctx % BLK == 0 and n_lat % BLK == 0 and n_lat % GRID_W == 0
    nbc = n_ctx // BLK
    tri, stri, bd16, eye, bd64, expand = _masks()
    rope_c, rope_sa, rope_sb = _rope_tables(n_ctx, n_lat)

    cvec = jnp.concatenate([c, c_ctx[None], jnp.zeros((8 - bsz - 1, d), F32)], axis=0)
    mod_all = _compute_mod(cvec, w_mod, b_mod).reshape(depth, 8, N_MOD, d)

    w1 = ffn_in.astype(BF16)
    w2 = ffn_out.astype(BF16)
    wo = w_out.astype(BF16)
    o_ssm = 784
    o_rw = o_ssm + 772
    o_att = o_rw + 1024
    w_proj = jnp.concatenate([_pad_cols(w_in[..., :o_ssm], GLA_W), _pad_cols(w_in[..., o_ssm:o_rw], SSM_W),
                              w_in[..., o_rw:o_att], w_in[..., o_att:]], axis=-1).astype(BF16)

    h = x
    for l in range(depth):
        gla_wd = _pad_rows(gla_w_dec[l], LANE)
        gla_bd = gla_b_dec[l][:, None, :]
        ssm_dtb = _rep(ssm_dt_bias[l], SSM_P)[:, None, :]
        ssm_an = _rep(-jnp.exp(ssm_a_log[l]), SSM_P)[:, None, :]
        rw_wd = _pad_rows(rwkv_w_dec[l], LANE)
        rw_wa = _pad_rows(rwkv_w_a[l], LANE, top=64)
        rw_w0 = rwkv_w0[l][:, None, :]

        h = _ffn(h, mod_all, norm_g[l, 0][None], w1, w2, l, 0, nbc, ctx=ctx if l == 0 else None)
        u_gla, u_ssm, qn, kt, vt, r, k2, v, g, kk, ab, tlw = _inproj(
            h, mod_all, norm_g[l, 1][None], w_proj, rope_c, rope_sa, rope_sb, jnp.tile(att_q_norm[l], ATT_HQ)[None],
            jnp.tile(att_k_norm[l], ATT_HKV)[None], bd64, rwkv_shift_mu[l], rwkv_a0[l][None], rw_wa, rwkv_w_g[l],
            rwkv_k_k[l][None], rwkv_k_a[l][None], l, nbc)
        gla_o, ssd_y, ssd_xs, rw_y = _scans(u_gla, gla_wd, gla_bd, u_ssm, ssm_conv_w[l], ssm_conv_b[l][None], ssm_dtb,
                                            ssm_an, expand, r, k2, v, kk, ab, tlw, rw_w0, rw_wd, tri, stri, bd16,
                                            eye, nbc)
        att_o = _attention(qn, kt, vt, nbc, n_ctx)

        h = _mix_out(h, mod_all, gla_o, u_gla, ssd_y, ssd_xs, u_ssm, rw_y, r, k2, v, g, att_o,
                     jnp.tile(gla_norm[l], GLA_H)[None], _rep(ssm_d[l], SSM_P)[None], ssm_norm[l][None],
                     rwkv_r_k[l].reshape(1, HEAD_W), rwkv_ln_g[l][None], rwkv_ln_b[l][None], bd64,
                     wo, l, nbc)
        h = _ffn(h, mod_all, norm_g[l, 2][None], w1, w2, l, 1, nbc, latent_only=l == depth - 1)
    return h
```
